```python
import jax, jax.numpy as jnp
from jax import lax
import numpy as np

D_MODEL = 2048
BATCH = 4
SEQ = 2048
DEPTH = 1
DEC_BATCH = 128
DEC_SEQ = 8
PAST_LEN = 16384
PAGE_SIZE = 128

N_META = 16
D_CONV = D_MODEL
CONV_W = 3
D_POOL = D_MODEL // 2
POOL_WINDOWS = (2, 4, 8, 16)
N_POOL_GROUPS = len(POOL_WINDOWS)
POOL_GROUP = D_POOL // N_POOL_GROUPS
POOL_OUT_GROUP = D_MODEL // N_POOL_GROUPS
MAX_WIN = max(POOL_WINDOWS)
D_FF = ((8 * D_MODEL // 3 + 255) // 256) * 256
D_IN = 3 * D_CONV + D_POOL + 2 * D_MODEL
SPLITS = (D_CONV, 2 * D_CONV, 3 * D_CONV, 3 * D_CONV + D_POOL, 3 * D_CONV + D_POOL + D_MODEL)
EPS = 1e-6

kernel_name = "gated_conv_pool_macaron_decoder_step"


def _rmsnorm(x, g):
    xf = x.astype(jnp.float32)
    r = lax.rsqrt(jnp.mean(xf * xf, axis=-1, keepdims=True) + EPS)
    return (xf * r).astype(x.dtype) * g


def _swiglu(x, wg, wu, wd):
    return (jax.nn.silu(x @ wg) * (x @ wu)) @ wd


def _short_conv(z, buf, w):
    L = z.shape[1]
    zp = jnp.concatenate([buf.astype(z.dtype), z], axis=1)
    y = w[0] * zp[:, 0:L]
    for k in range(1, CONV_W):
        y = y + w[k] * zp[:, k:k + L]
    return y, zp[:, -(CONV_W - 1):]


def _multiscale_pool(z, buf, pos):
    L = z.shape[1]
    P = MAX_WIN - 1
    zp = jnp.concatenate([buf.astype(z.dtype), z], axis=1)
    cs = jnp.cumsum(zp.astype(jnp.float32), axis=1)
    cs = jnp.concatenate([jnp.zeros_like(cs[:, :1]), cs], axis=1)
    outs = []
    for g, k in enumerate(POOL_WINDOWS):
        c0, c1 = g * POOL_GROUP, (g + 1) * POOL_GROUP
        s = cs[:, P + 1:P + 1 + L, c0:c1] - cs[:, P + 1 - k:P + 1 - k + L, c0:c1]
        cnt = jnp.minimum(k, pos + 1).astype(jnp.float32)[None, :, None]
        outs.append(s / cnt - z[:, :, c0:c1].astype(jnp.float32))
    pooled = jnp.stack(outs, axis=2).astype(z.dtype)
    return pooled, zp[:, -P:]


def _layer(x, conv_buf, pool_buf, pos, norm_ffn1, w1g, w1u, w1d, norm_mix, w_in, conv_w,
           w_conv_out, w_pool_group, pool_scale, w_o, norm_ffn2, w2g, w2u, w2d):
    b, L, _ = x.shape
    h = x + 0.5 * _swiglu(_rmsnorm(x, norm_ffn1), w1g, w1u, w1d)
    u = _rmsnorm(h, norm_mix)
    proj = u @ w_in
    b_gate, c_gate, v, z_pool, g_conv, g_pool = jnp.split(proj, SPLITS, axis=-1)
    conv_y, conv_state = _short_conv(c_gate * v, conv_buf, conv_w)
    y_conv = (b_gate * conv_y) @ w_conv_out
    pooled, pool_state = _multiscale_pool(z_pool, pool_buf, pos)
    y_pool = jnp.einsum('blgc,gcd->blgd', pooled, w_pool_group).reshape(b, L, D_MODEL) * pool_scale
    merged = jax.nn.sigmoid(g_conv) * y_conv + jax.nn.sigmoid(g_pool) * y_pool
    h = h + merged @ w_o
    h = h + 0.5 * _swiglu(_rmsnorm(h, norm_ffn2), w2g, w2u, w2d)
    return h, conv_state, pool_state


def setup_inputs(seed: int = 0) -> dict:
    key = jax.random.key(seed)
    ks = jax.random.split(key, 24)
    f32 = jnp.float32

    def nrm(k, shape, fan_in):
        return jax.random.normal(k, shape, f32) * (fan_in ** -0.5)

    def gain(k, shape):
        return 1.0 + 0.02 * jax.random.normal(k, shape, f32)

    return {
        "x_prompt": jax.random.normal(ks[0], (BATCH, SEQ, D_MODEL), f32),
        "x_sample": jax.random.normal(ks[1], (DEC_BATCH, DEC_SEQ, D_MODEL), f32),
        "state_conv": jax.random.normal(ks[2], (DEPTH, DEC_BATCH, CONV_W - 1, D_CONV), f32),
        "state_pool": jax.random.normal(ks[3], (DEPTH, DEC_BATCH, MAX_WIN - 1, D_POOL), f32),
        "meta_tokens": jax.random.normal(ks[4], (N_META, D_MODEL), f32),
        "norm_ffn1": gain(ks[5], (DEPTH, D_MODEL)),
        "w_ffn1_gate": nrm(ks[6], (DEPTH, D_MODEL, D_FF), D_MODEL),
        "w_ffn1_up": nrm(ks[7], (DEPTH, D_MODEL, D_FF), D_MODEL),
        "w_ffn1_down": nrm(ks[8], (DEPTH, D_FF, D_MODEL), D_FF),
        "norm_mix": gain(ks[9], (DEPTH, D_MODEL)),
        "w_in": nrm(ks[10], (DEPTH, D_MODEL, D_IN), D_MODEL),
        "conv_w": nrm(ks[11], (DEPTH, CONV_W, D_CONV), CONV_W),
        "w_conv_out": nrm(ks[12], (DEPTH, D_CONV, D_MODEL), D_CONV),
        "w_pool_group": nrm(ks[13], (DEPTH, N_POOL_GROUPS, POOL_GROUP, POOL_OUT_GROUP), POOL_GROUP),
        "pool_scale": gain(ks[14], (DEPTH, D_MODEL)),
        "w_o": nrm(ks[15], (DEPTH, D_MODEL, D_MODEL), D_MODEL),
        "norm_ffn2": gain(ks[16], (DEPTH, D_MODEL)),
        "w_ffn2_gate": nrm(ks[17], (DEPTH, D_MODEL, D_FF), D_MODEL),
        "w_ffn2_up": nrm(ks[18], (DEPTH, D_MODEL, D_FF), D_MODEL),
        "w_ffn2_down": nrm(ks[19], (DEPTH, D_FF, D_MODEL), D_FF),
        "norm_final": gain(ks[20], (D_MODEL,)),
    }


def reference(x_prompt, x_sample, state_conv, state_pool, meta_tokens, norm_ffn1, w_ffn1_gate,
              w_ffn1_up, w_ffn1_down, norm_mix, w_in, conv_w, w_conv_out, w_pool_group,
              pool_scale, w_o, norm_ffn2, w_ffn2_gate, w_ffn2_up, w_ffn2_down, norm_final):
    meta = jnp.broadcast_to(meta_tokens.astype(x_prompt.dtype)[None], (BATCH, N_META, D_MODEL))
    hp = jnp.concatenate([meta, x_prompt], axis=1)
    pos_p = jnp.arange(N_META + SEQ, dtype=jnp.int32)
    pos_s = PAST_LEN + jnp.arange(DEC_SEQ, dtype=jnp.int32)
    hs = x_sample
    conv_p, pool_p, conv_s, pool_s = [], [], [], []
    for l in range(DEPTH):
        lw = (norm_ffn1[l], w_ffn1_gate[l], w_ffn1_up[l], w_ffn1_down[l], norm_mix[l], w_in[l],
              conv_w[l], w_conv_out[l], w_pool_group[l], pool_scale[l], w_o[l], norm_ffn2[l],
              w_ffn2_gate[l], w_ffn2_up[l], w_ffn2_down[l])
        zbuf_c = jnp.zeros((BATCH, CONV_W - 1, D_CONV), hp.dtype)
        zbuf_p = jnp.zeros((BATCH, MAX_WIN - 1, D_POOL), hp.dtype)
        hp, cp, pp = _layer(hp, zbuf_c, zbuf_p, pos_p, *lw)
        hs, cs_, ps_ = _layer(hs, state_conv[l], state_pool[l], pos_s, *lw)
        conv_p.append(cp)
        pool_p.append(pp)
        conv_s.append(cs_)
        pool_s.append(ps_)
    y_prompt = _rmsnorm(hp[:, N_META:], norm_final)
    y_sample = _rmsnorm(hs, norm_final)
    new_conv_prompt = jnp.stack(conv_p, axis=0)
    new_pool_prompt = jnp.stack(pool_p, axis=0)
    new_conv_sample = jnp.stack(conv_s, axis=0)
    new_pool_sample = jnp.stack(pool_s, axis=0)
    return (y_prompt, y_sample, new_conv_prompt, new_pool_prompt, new_conv_sample, new_pool_sample)
```

```python
import functools

import jax
import jax.numpy as jnp
from jax import lax
from jax.experimental import pallas as pl
from jax.experimental.pallas import tpu as pltpu

F32 = jnp.float32
BF16 = jnp.bfloat16

EPS = 1e-6
N_META = 16
CONV_W = 3
POOL_WINDOWS = (2, 4, 8, 16)
MAX_WIN = max(POOL_WINDOWS)
N_GROUPS = len(POOL_WINDOWS)

F32_SUBLANES = 8
V7X_VMEM_LIMIT_BYTES = 56 * 1024 * 1024


def _params(semantics):
    return pltpu.CompilerParams(dimension_semantics=semantics,
                                vmem_limit_bytes=V7X_VMEM_LIMIT_BYTES)


def _dot(a, b):
    return jnp.dot(a, b, preferred_element_type=F32)


def _rms(x, g):
    ms = jnp.mean(x * x, axis=-1, keepdims=True)
    return (x * lax.rsqrt(ms + EPS)) * g


def _resident(shape, index_map):
    return pl.BlockSpec(shape, index_map, pipeline_mode=pl.Buffered(1))


def _ffn_body(x_ref, g_ref, wg_ref, wu_ref, wd_ref, g2_ref, *refs, emit_h):
    if emit_h:
        acc_ref, n_ref, xn_ref = refs
    else:
        acc_ref, xn_ref = refs
        n_ref = acc_ref
    f = pl.program_id(1)

    @pl.when(f == 0)
    def _():
        xn_ref[...] = _rms(x_ref[...], g_ref[...]).astype(BF16)

    xn = xn_ref[...]
    a = _dot(xn, wg_ref[...])
    b = _dot(xn, wu_ref[...])
    act = ((a * jax.nn.sigmoid(a)) * b).astype(BF16)
    part = _dot(act, wd_ref[...])

    @pl.when(f == 0)
    def _():
        acc_ref[...] = part

    @pl.when(f > 0)
    def _():
        acc_ref[...] += part

    @pl.when(f == pl.num_programs(1) - 1)
    def _():
        h = x_ref[...] + 0.5 * acc_ref[...]
        if emit_h:
            acc_ref[...] = h
        n_ref[...] = _rms(h, g2_ref[...]).astype(n_ref.dtype)


def _ffn(x, g, wg, wu, wd, g2, *, tm, tf, emit_h, name):
    rows, d = x.shape
    d_ff = wg.shape[1]
    n_i, n_f = rows // tm, d_ff // tf
    assert rows == n_i * tm and d_ff == n_f * tf
    row_spec = (lambda s, m: _resident(s, m)) if n_i == 1 else pl.BlockSpec
    row = lambda i, f: (i, 0)
    in_specs = [
        row_spec((tm, d), row),
        pl.BlockSpec((1, d), lambda i, f: (0, 0)),
        pl.BlockSpec((d, tf), lambda i, f: (0, f)),
        pl.BlockSpec((d, tf), lambda i, f: (0, f)),
        pl.BlockSpec((tf, d), lambda i, f: (f, 0)),
        pl.BlockSpec((1, d), lambda i, f: (0, 0)),
    ]
    if emit_h:
        out_shape = (jax.ShapeDtypeStruct((rows, d), F32), jax.ShapeDtypeStruct((rows, d), BF16))
        out_specs = (row_spec((tm, d), row), row_spec((tm, d), row))
    else:
        out_shape = jax.ShapeDtypeStruct((rows, d), F32)
        out_specs = row_spec((tm, d), row)
    return pl.pallas_call(
        functools.partial(_ffn_body, emit_h=emit_h),
        grid=(n_i, n_f),
        in_specs=in_specs,
        out_specs=out_specs,
        out_shape=out_shape,
        scratch_shapes=[pltpu.VMEM((tm, d), BF16)],
        compiler_params=_params(("arbitrary", "arbitrary")),
        name=name,
    )(x, g, wg, wu, wd, g2)


def _win_spec_list(d, d_pool, n_j, tc, tz):
    conv_blocks = d // tc
    z0 = (3 * d) // tz
    gc0 = (3 * d + d_pool) // tc
    gp0 = gc0 + conv_blocks
    assert (3 * d + d_pool) % tc == 0
    mk = lambda w, off: pl.BlockSpec((d, w), lambda j, *_: (0, off + j))
    return [mk(tc, 0), mk(tc, conv_blocks), mk(tc, 2 * conv_blocks), mk(tz, z0), mk(tc, gc0), mk(tc, gp0)]


def _conv3(w_ref, ecv_ref, cur, rows):
    base = F32_SUBLANES
    w = w_ref[...]
    y = w[0:1] * ecv_ref[pl.ds(base - 2, rows), :]
    y = y + w[1:2] * ecv_ref[pl.ds(base - 1, rows), :]
    return y + w[2:3] * cur


def _window_sums(ez_ref, ep_ref, rows, g):
    base = F32_SUBLANES
    zeros = jnp.zeros((base, ez_ref.shape[1]), F32)

    def shift_add(dst, src, s):
        dst[pl.ds(base, rows), :] = src[pl.ds(base, rows), :] + src[pl.ds(base - s, rows), :]

    ep_ref[pl.ds(0, base), :] = zeros
    shift_add(ep_ref, ez_ref, 1)

    @pl.when(g >= 1)
    def _():
        shift_add(ez_ref, ep_ref, 2)

    @pl.when(g >= 2)
    def _():
        shift_add(ep_ref, ez_ref, 4)

    @pl.when(g >= 3)
    def _():
        shift_add(ez_ref, ep_ref, 8)


def _inv_window(g):
    return jnp.where(g == 0, 0.5, jnp.where(g == 1, 0.25, jnp.where(g == 2, 0.125, 0.0625))).astype(F32)


def _mix_prompt_body(u_ref, wb_ref, wc_ref, wv_ref, wz_ref, wgc_ref, wgp_ref, cw_ref, cinit_ref,
                     zinit_ref, bc_ref, pooled_ref, sgc_ref, sgp_ref, cvlast_ref, zlast_ref,
                     ecv_ref, ez_ref, ep_ref, zc_ref, *, tm, tiles_per_seq, n_j):
    base = F32_SUBLANES
    j = pl.program_id(0)
    i = pl.program_id(1)
    t = i % tiles_per_seq
    b = i // tiles_per_seq
    g = j // (n_j // N_GROUPS)
    u = u_ref[...]

    cv = _dot(u, wc_ref[...]) * _dot(u, wv_ref[...])

    @pl.when(t == 0)
    def _():
        ecv_ref[pl.ds(0, base), :] = cinit_ref[...]

    ecv_ref[pl.ds(base, tm), :] = cv
    y = _conv3(cw_ref, ecv_ref, cv, tm)
    bc_ref[...] = (_dot(u, wb_ref[...]) * y).astype(BF16)

    @pl.when(t == tiles_per_seq - 1)
    def _():
        cvlast_ref[b] = ecv_ref[pl.ds(base + tm - (CONV_W - 1), CONV_W - 1), :]

    ecv_ref[pl.ds(0, base), :] = ecv_ref[pl.ds(tm, base), :]

    z = _dot(u, wz_ref[...])

    @pl.when(t == 0)
    def _():
        zc_ref[...] = zinit_ref[...]

    ez_ref[pl.ds(0, base), :] = jnp.zeros((base, ez_ref.shape[1]), F32)
    ez_ref[pl.ds(base, MAX_WIN), :] = zc_ref[...]
    ez_ref[pl.ds(base + MAX_WIN, tm), :] = z
    zc_ref[...] = ez_ref[pl.ds(base + tm, MAX_WIN), :]

    @pl.when(t == tiles_per_seq - 1)
    def _():
        zlast_ref[b] = zc_ref[...]

    _window_sums(ez_ref, ep_ref, MAX_WIN + tm, g)
    inv_k = _inv_window(g)

    def emit(src_ref):
        s = src_ref[pl.ds(base + MAX_WIN, tm), :]
        pooled_ref[...] = (s * inv_k - z).astype(BF16)

    pl.when(g % 2 == 0)(lambda: emit(ep_ref))
    pl.when(g % 2 == 1)(lambda: emit(ez_ref))

    sgc_ref[...] = jax.nn.sigmoid(_dot(u, wgc_ref[...]))
    sgp_ref[...] = jax.nn.sigmoid(_dot(u, wgp_ref[...]))


def _mix_prompt(u, w_in, conv_w, cinit, zinit, *, n_seq, tm, n_j):
    rows, d = u.shape
    d_pool = zinit.shape[1]
    tc, tz = d // n_j, d_pool // n_j
    seq = rows // n_seq
    tiles_per_seq = seq // tm
    n_i = rows // tm
    assert seq == tiles_per_seq * tm and n_j % N_GROUPS == 0
    blk = lambda w: pl.BlockSpec((tm, w), lambda j, i: (i, j))
    in_specs = ([pl.BlockSpec((tm, d), lambda j, i: (i, 0))]
                + _win_spec_list(d, d_pool, n_j, tc, tz)
                + [pl.BlockSpec((CONV_W, tc), lambda j, i: (0, j)),
                   pl.BlockSpec((F32_SUBLANES, tc), lambda j, i: (0, j)),
                   pl.BlockSpec((MAX_WIN, tz), lambda j, i: (0, j))])
    out_shape = (jax.ShapeDtypeStruct((rows, d), BF16),
                 jax.ShapeDtypeStruct((rows, d_pool), BF16),
                 jax.ShapeDtypeStruct((rows, d), F32),
                 jax.ShapeDtypeStruct((rows, d), F32),
                 jax.ShapeDtypeStruct((n_seq, CONV_W - 1, d), F32),
                 jax.ShapeDtypeStruct((n_seq, MAX_WIN, d_pool), F32))
    out_specs = (blk(tc), blk(tz), blk(tc), blk(tc),
                 pl.BlockSpec((n_seq, CONV_W - 1, tc), lambda j, i: (0, 0, j)),
                 pl.BlockSpec((n_seq, MAX_WIN, tz), lambda j, i: (0, 0, j)))
    scratch = [pltpu.VMEM((F32_SUBLANES + tm, tc), F32),
               pltpu.VMEM((F32_SUBLANES + MAX_WIN + tm, tz), F32),
               pltpu.VMEM((F32_SUBLANES + MAX_WIN + tm, tz), F32),
               pltpu.VMEM((MAX_WIN, tz), F32)]
    return pl.pallas_call(
        functools.partial(_mix_prompt_body, tm=tm, tiles_per_seq=tiles_per_seq, n_j=n_j),
        grid=(n_j, n_i),
        in_specs=in_specs,
        out_specs=out_specs,
        out_shape=out_shape,
        scratch_shapes=scratch,
        compiler_params=_params(("arbitrary", "arbitrary")),
        name="mix_prompt",
    )(u, w_in, w_in, w_in, w_in, w_in, w_in, conv_w, cinit, zinit)


def _mix_sample_body(u_ref, wb_ref, wc_ref, wv_ref, wz_ref, wgc_ref, wgp_ref, cw_ref, cst_ref,
                     zst_ref, bc_ref, pooled_ref, sgc_ref, sgp_ref, cvall_ref, zall_ref,
                     ecv_ref, ez_ref, ep_ref, *, n_seq, n_j):
    base = F32_SUBLANES
    seq = F32_SUBLANES
    ns = n_seq * seq
    j = pl.program_id(0)
    g = j // (n_j // N_GROUPS)
    u = u_ref[...]
    us = u[:ns]
    tc = cvall_ref.shape[1]
    tz = zall_ref.shape[1]

    cv_all = _dot(u, wc_ref[...]) * _dot(u, wv_ref[...])
    cvall_ref[...] = cv_all
    ext = jnp.concatenate([cst_ref[...], cv_all[:ns].reshape(n_seq, seq, tc)], axis=1)
    ext = ext.reshape(2 * ns, tc)
    ecv_ref[pl.ds(0, base), :] = jnp.zeros((base, tc), F32)
    ecv_ref[pl.ds(base, 2 * ns), :] = ext
    y = _conv3(cw_ref, ecv_ref, ext, 2 * ns)
    y = y.reshape(n_seq, 2 * seq, tc)[:, seq:, :].reshape(ns, tc)
    bc_ref[...] = (_dot(us, wb_ref[...]) * y).astype(BF16)

    z_all = _dot(u, wz_ref[...])
    zall_ref[...] = z_all
    z = z_all[:ns]
    per = MAX_WIN + seq
    ext = jnp.concatenate([zst_ref[...], z.reshape(n_seq, seq, tz)], axis=1).reshape(n_seq * per, tz)
    ez_ref[pl.ds(0, base), :] = jnp.zeros((base, tz), F32)
    ez_ref[pl.ds(base, n_seq * per), :] = ext
    _window_sums(ez_ref, ep_ref, n_seq * per, g)
    inv_k = _inv_window(g)

    def emit(src_ref):
        s = src_ref[pl.ds(base, n_seq * per), :].reshape(n_seq, per, tz)[:, MAX_WIN:, :].reshape(ns, tz)
        pooled_ref[...] = (s * inv_k - z).astype(BF16)

    pl.when(g % 2 == 0)(lambda: emit(ep_ref))
    pl.when(g % 2 == 1)(lambda: emit(ez_ref))

    sgc_ref[...] = jax.nn.sigmoid(_dot(us, wgc_ref[...]))
    sgp_ref[...] = jax.nn.sigmoid(_dot(us, wgp_ref[...]))


def _mix_sample(u, w_in, conv_w, cst, zst, *, n_j):
    rows, d = u.shape
    n_seq, _, d_pool = zst.shape
    ns = n_seq * F32_SUBLANES
    tc, tz = d // n_j, d_pool // n_j
    assert n_j % N_GROUPS == 0 and cst.shape == (n_seq, F32_SUBLANES, d)
    col = lambda r, w: pl.BlockSpec((r, w), lambda j: (0, j))
    in_specs = ([_resident((rows, d), lambda j: (0, 0))]
                + _win_spec_list(d, d_pool, n_j, tc, tz)
                + [col(CONV_W, tc),
                   pl.BlockSpec((n_seq, F32_SUBLANES, tc), lambda j: (0, 0, j)),
                   pl.BlockSpec((n_seq, MAX_WIN, tz), lambda j: (0, 0, j))])
    out_shape = (jax.ShapeDtypeStruct((ns, d), BF16),
                 jax.ShapeDtypeStruct((ns, d_pool), BF16),
                 jax.ShapeDtypeStruct((ns, d), F32),
                 jax.ShapeDtypeStruct((ns, d), F32),
                 jax.ShapeDtypeStruct((rows, d), F32),
                 jax.ShapeDtypeStruct((rows, d_pool), F32))
    out_specs = (col(ns, tc), col(ns, tz), col(ns, tc), col(ns, tc), col(rows, tc), col(rows, tz))
    per = MAX_WIN + F32_SUBLANES
    scratch = [pltpu.VMEM((F32_SUBLANES + 2 * ns, tc), F32),
               pltpu.VMEM((F32_SUBLANES + n_seq * per, tz), F32),
               pltpu.VMEM((F32_SUBLANES + n_seq * per, tz), F32)]
    return pl.pallas_call(
        functools.partial(_mix_sample_body, n_seq=n_seq, n_j=n_j),
        grid=(n_j,),
        in_specs=in_specs,
        out_specs=out_specs,
        out_shape=out_shape,
        scratch_shapes=scratch,
        compiler_params=_params(("arbitrary",)),
        name="mix_sample",
    )(u, w_in, w_in, w_in, w_in, w_in, w_in, conv_w, cst, zst)


def _outproj_body(bc_ref, pooled_ref, sgc_ref, sgp_ref, h_ref, wco_ref, wpg_ref, ps_ref, wo_ref,
                  h2_ref, merged_ref, *, n_n, tn):
    n = pl.program_id(1)
    y_conv = _dot(bc_ref[...], wco_ref[...])
    y_pool = _dot(pooled_ref[...], wpg_ref[0]) * ps_ref[...]
    m = (sgc_ref[...] * y_conv + sgp_ref[...] * y_pool).astype(BF16)
    for c in range(n_n):
        @pl.when(n == c)
        def _(c=c):
            merged_ref[:, c * tn:(c + 1) * tn] = m

    @pl.when(n == n_n - 1)
    def _():
        h2_ref[...] = h_ref[...] + _dot(merged_ref[...], wo_ref[...])


def _outproj(bc, pooled, sgc, sgp, h, wco, wpg, pscale, wo, *, tm, name):
    rows, d = bc.shape
    n_n, pool_in, tn = wpg.shape
    n_i = rows // tm
    assert rows == n_i * tm and d == n_n * tn
    in_specs = [
        pl.BlockSpec((tm, d), lambda i, n: (i, 0)),
        pl.BlockSpec((tm, pool_in), lambda i, n: (i, n)),
        pl.BlockSpec((tm, tn), lambda i, n: (i, n)),
        pl.BlockSpec((tm, tn), lambda i, n: (i, n)),
        pl.BlockSpec((tm, d), lambda i, n: (i, 0)),
        pl.BlockSpec((d, tn), lambda i, n: (0, n)),
        pl.BlockSpec((1, pool_in, tn), lambda i, n: (n, 0, 0)),
        pl.BlockSpec((1, tn), lambda i, n: (0, n)),
        _resident((d, d), lambda i, n: (0, 0)),
    ]
    return pl.pallas_call(
        functools.partial(_outproj_body, n_n=n_n, tn=tn),
        grid=(n_i, n_n),
        in_specs=in_specs,
        out_specs=pl.BlockSpec((tm, d), lambda i, n: (i, 0)),
        out_shape=jax.ShapeDtypeStruct((rows, d), F32),
        scratch_shapes=[pltpu.VMEM((tm, d), BF16)],
        compiler_params=_params(("arbitrary", "arbitrary")),
        name=name,
    )(bc, pooled, sgc, sgp, h, wco, wpg, pscale, wo)


def kernel(x_prompt, x_sample, state_conv, state_pool, meta_tokens, norm_ffn1, w_ffn1_gate,
           w_ffn1_up, w_ffn1_down, norm_mix, w_in, conv_w, w_conv_out, w_pool_group, pool_scale,
           w_o, norm_ffn2, w_ffn2_gate, w_ffn2_up, w_ffn2_down, norm_final):
    batch, seq, d = x_prompt.shape
    dec_batch, dec_seq, _ = x_sample.shape
    depth = w_in.shape[0]
    d_pool = state_pool.shape[-1]
    assert depth == 1 and dec_seq == F32_SUBLANES and meta_tokens.shape[0] == N_META == MAX_WIN
    assert state_conv.shape[2] == CONV_W - 1 and state_pool.shape[2] == MAX_WIN - 1
    n_s = dec_batch * dec_seq

    bf = lambda w: w[0].astype(BF16)
    row = lambda v: v.reshape(1, d)
    wg1, wu1, wd1 = bf(w_ffn1_gate), bf(w_ffn1_up), bf(w_ffn1_down)
    wg2, wu2, wd2 = bf(w_ffn2_gate), bf(w_ffn2_up), bf(w_ffn2_down)
    win, wco, wpg, wo = bf(w_in), bf(w_conv_out), bf(w_pool_group), bf(w_o)
    g1, gm, g2, gf = row(norm_ffn1[0]), row(norm_mix[0]), row(norm_ffn2[0]), row(norm_final)
    pscale = row(pool_scale[0])
    cw = conv_w[0]

    xs = jnp.concatenate([x_sample.reshape(n_s, d), meta_tokens.astype(F32)], axis=0)
    h_s, u_s = _ffn(xs, g1, wg1, wu1, wd1, gm, tm=xs.shape[0], tf=256, emit_h=True, name="ffn1_sample")
    cst = jnp.pad(state_conv[0], ((0, 0), (F32_SUBLANES - (CONV_W - 1), 0), (0, 0)))
    zst = jnp.pad(state_pool[0], ((0, 0), (1, 0), (0, 0)))
    bc_s, pooled_s, sgc_s, sgp_s, cv_all, z_all = _mix_sample(u_s, win, cw, cst, zst, n_j=8)
    h2_s = _outproj(bc_s, pooled_s, sgc_s, sgp_s, h_s, wco, wpg, pscale, wo, tm=512, name="outproj_sample")
    y_s = _ffn(h2_s, g2, wg2, wu2, wd2, gf, tm=512, tf=512, emit_h=False, name="ffn2_sample")

    cv_s = cv_all[:n_s].reshape(dec_batch, dec_seq, d)
    new_conv_sample = cv_s[:, dec_seq - (CONV_W - 1):]
    new_pool_sample = jnp.concatenate(
        [state_pool[0][:, dec_seq:], z_all[:n_s].reshape(dec_batch, dec_seq, d_pool)], axis=1)

    cinit = jnp.pad(cv_all[n_s + N_META - (CONV_W - 1):], ((F32_SUBLANES - (CONV_W - 1), 0), (0, 0)))
    zinit = z_all[n_s:]
    xp = x_prompt.reshape(batch * seq, d)
    h_p, u_p = _ffn(xp, g1, wg1, wu1, wd1, gm, tm=512, tf=512, emit_h=True, name="ffn1_prompt")
    bc_p, pooled_p, sgc_p, sgp_p, cvlast, zlast = _mix_prompt(u_p, win, cw, cinit, zinit,
                                                              n_seq=batch, tm=512, n_j=4)
    h2_p = _outproj(bc_p, pooled_p, sgc_p, sgp_p, h_p, wco, wpg, pscale, wo, tm=512, name="outproj_prompt")
    y_p = _ffn(h2_p, g2, wg2, wu2, wd2, gf, tm=512, tf=512, emit_h=False, name="ffn2_prompt")

    return (y_p.reshape(batch, seq, d),
            y_s.reshape(dec_batch, dec_seq, d),
            cvlast[None],
            zlast[:, 1:][None],
            new_conv_sample[None],
            new_pool_sample[None])
```

```python
import functools

import jax
import jax.numpy as jnp
from jax import lax
from jax.experimental import pallas as pl
from jax.experimental.pallas import tpu as pltpu

F32 = jnp.float32
BF16 = jnp.bfloat16

EPS = 1e-6
N_META = 16
CONV_W = 3
POOL_WINDOWS = (2, 4, 8, 16)
MAX_WIN = max(POOL_WINDOWS)
N_GROUPS = len(POOL_WINDOWS)

F32_SUBLANES = 8
V7X_VMEM_LIMIT_BYTES = 60 * 1024 * 1024


def _params(semantics):
    return pltpu.CompilerParams(dimension_semantics=semantics,
                                vmem_limit_bytes=V7X_VMEM_LIMIT_BYTES)


def _dot(a, b):
    return jnp.dot(a, b, preferred_element_type=F32)


def _rms(x, g):
    ms = jnp.mean(x * x, axis=-1, keepdims=True)
    return (x * lax.rsqrt(ms + EPS)) * g


def _resident(shape, index_map):
    return pl.BlockSpec(shape, index_map, pipeline_mode=pl.Buffered(1))


def _weight(w_ref, copy_ref):
    w = w_ref[...].astype(BF16)
    if copy_ref is not None:
        copy_ref[...] = w
    return w


def _weight_copies(arrays, specs, emit):
    if not emit:
        return [], []
    return [jax.ShapeDtypeStruct(a.shape, BF16) for a in arrays], list(specs)


def _ffn_body(x_ref, g_ref, wg_ref, wu_ref, wd_ref, g2_ref, *refs, emit_h, emit_w):
    refs = list(refs)
    acc_ref = refs.pop(0)
    n_ref = refs.pop(0) if emit_h else acc_ref
    copies = [refs.pop(0) for _ in range(3)] if emit_w else [None] * 3
    (xn_ref,) = refs
    f = pl.program_id(1)

    @pl.when(f == 0)
    def _():
        xn_ref[...] = _rms(x_ref[...], g_ref[...]).astype(BF16)
        acc_ref[...] = jnp.zeros(acc_ref.shape, F32)

    xn = xn_ref[...]
    a = _dot(xn, _weight(wg_ref, copies[0]))
    b = _dot(xn, _weight(wu_ref, copies[1]))
    act = ((a * jax.nn.sigmoid(a)) * b).astype(BF16)
    acc_ref[...] += _dot(act, _weight(wd_ref, copies[2]))

    @pl.when(f == pl.num_programs(1) - 1)
    def _():
        h = x_ref[...] + 0.5 * acc_ref[...]
        if emit_h:
            acc_ref[...] = h
        n_ref[...] = _rms(h, g2_ref[...]).astype(n_ref.dtype)


def _ffn(x, g, wg, wu, wd, g2, *, tm, tf, emit_h, name):
    rows, d = x.shape
    d_ff = wg.shape[1]
    n_i, n_f = rows // tm, d_ff // tf
    emit_w = wg.dtype == F32
    assert rows == n_i * tm and d_ff == n_f * tf
    assert n_i == 1 or not emit_w
    row_spec = _resident if n_i == 1 else pl.BlockSpec
    row = lambda i, f: (i, 0)
    w_specs = [pl.BlockSpec((d, tf), lambda i, f: (0, f)),
               pl.BlockSpec((d, tf), lambda i, f: (0, f)),
               pl.BlockSpec((tf, d), lambda i, f: (f, 0))]
    in_specs = ([row_spec((tm, d), row), pl.BlockSpec((1, d), lambda i, f: (0, 0))] + w_specs
                + [pl.BlockSpec((1, d), lambda i, f: (0, 0))])
    out_shape = [jax.ShapeDtypeStruct((rows, d), F32)]
    out_specs = [row_spec((tm, d), row)]
    if emit_h:
        out_shape.append(jax.ShapeDtypeStruct((rows, d), BF16))
        out_specs.append(row_spec((tm, d), row))
    w_shapes, w_out_specs = _weight_copies((wg, wu, wd), w_specs, emit_w)
    return pl.pallas_call(
        functools.partial(_ffn_body, emit_h=emit_h, emit_w=emit_w),
        grid=(n_i, n_f),
        in_specs=in_specs,
        out_specs=out_specs + w_out_specs,
        out_shape=out_shape + w_shapes,
        scratch_shapes=[pltpu.VMEM((tm, d), BF16)],
        compiler_params=_params(("arbitrary", "arbitrary")),
        name=name,
    )(x, g, wg, wu, wd, g2)


def _conv3(w_ref, ext):
    w = w_ref[...]
    y = w[0:1] * pltpu.roll(ext, 2, axis=0)
    y = y + w[1:2] * pltpu.roll(ext, 1, axis=0)
    return y + w[2:3] * ext


def _window_sums(e, g):
    p = e + pltpu.roll(e, 1, axis=0)
    for level, shift in ((1, 2), (2, 4), (3, 8)):
        p = jnp.where(g >= level, p + pltpu.roll(p, shift, axis=0), p)
    return p


def _inv_window(g):
    return jnp.where(g == 0, 0.5, jnp.where(g == 1, 0.25, jnp.where(g == 2, 0.125, 0.0625))).astype(F32)


def _mix_prompt_body(u_ref, wb_ref, wc_ref, wv_ref, wz_ref, wgc_ref, wgp_ref, cw_ref, cinit_ref,
                     zinit_ref, bc_ref, pooled_ref, sgc_ref, sgp_ref, cvlast_ref, zlast_ref,
                     cpast_ref, zpast_ref, *, tm, tiles_per_seq, n_j):
    j = pl.program_id(0)
    i = pl.program_id(1)
    t = i % tiles_per_seq
    b = i // tiles_per_seq
    g = j // (n_j // N_GROUPS)
    u = u_ref[...]

    @pl.when(t == 0)
    def _():
        cpast_ref[...] = cinit_ref[...]
        zpast_ref[...] = zinit_ref[...]

    cv = _dot(u, wc_ref[...]) * _dot(u, wv_ref[...])
    ext = jnp.concatenate([cpast_ref[...], cv], axis=0)
    y = _conv3(cw_ref, ext)[F32_SUBLANES:]
    bc_ref[...] = (_dot(u, wb_ref[...]) * y).astype(BF16)
    cpast_ref[...] = cv[tm - F32_SUBLANES:]

    z = _dot(u, wz_ref[...])
    sums = _window_sums(jnp.concatenate([zpast_ref[...], z], axis=0), g)[MAX_WIN:]
    pooled_ref[...] = (sums * _inv_window(g) - z).astype(BF16)
    zpast_ref[...] = z[tm - MAX_WIN:]

    @pl.when(t == tiles_per_seq - 1)
    def _():
        cvlast_ref[b] = cpast_ref[pl.ds(F32_SUBLANES - (CONV_W - 1), CONV_W - 1), :]
        zlast_ref[b] = zpast_ref[...]

    sgc_ref[...] = jax.nn.sigmoid(_dot(u, wgc_ref[...]))
    sgp_ref[...] = jax.nn.sigmoid(_dot(u, wgp_ref[...]))


def _mix_prompt(u, w_groups, conv_w, cinit, zinit, *, n_seq, tm, n_j):
    rows, d = u.shape
    d_pool = zinit.shape[1]
    tc, tz = d // n_j, d_pool // n_j
    seq = rows // n_seq
    tiles_per_seq = seq // tm
    n_i = rows // tm
    assert seq == tiles_per_seq * tm and n_j % N_GROUPS == 0 and tm >= MAX_WIN
    blk = lambda w: pl.BlockSpec((tm, w), lambda j, i: (i, j))
    wcol = lambda w: pl.BlockSpec((d, w), lambda j, i: (0, j))
    in_specs = ([pl.BlockSpec((tm, d), lambda j, i: (i, 0))]
                + [wcol(tc), wcol(tc), wcol(tc), wcol(tz), wcol(tc), wcol(tc)]
                + [pl.BlockSpec((CONV_W, tc), lambda j, i: (0, j)),
                   pl.BlockSpec((F32_SUBLANES, tc), lambda j, i: (0, j)),
                   pl.BlockSpec((MAX_WIN, tz), lambda j, i: (0, j))])
    out_shape = (jax.ShapeDtypeStruct((rows, d), BF16),
                 jax.ShapeDtypeStruct((rows, d_pool), BF16),
                 jax.ShapeDtypeStruct((rows, d), F32),
                 jax.ShapeDtypeStruct((rows, d), F32),
                 jax.ShapeDtypeStruct((n_seq, CONV_W - 1, d), F32),
                 jax.ShapeDtypeStruct((n_seq, MAX_WIN, d_pool), F32))
    out_specs = (blk(tc), blk(tz), blk(tc), blk(tc),
                 pl.BlockSpec((n_seq, CONV_W - 1, tc), lambda j, i: (0, 0, j)),
                 pl.BlockSpec((n_seq, MAX_WIN, tz), lambda j, i: (0, 0, j)))
    scratch = [pltpu.VMEM((F32_SUBLANES, tc), F32), pltpu.VMEM((MAX_WIN, tz), F32)]
    return pl.pallas_call(
        functools.partial(_mix_prompt_body, tm=tm, tiles_per_seq=tiles_per_seq, n_j=n_j),
        grid=(n_j, n_i),
        in_specs=in_specs,
        out_specs=out_specs,
        out_shape=out_shape,
        scratch_shapes=scratch,
        compiler_params=_params(("arbitrary", "arbitrary")),
        name="mix_prompt",
    )(u, *w_groups, conv_w, cinit, zinit)


def _mix_sample_body(u_ref, wb_ref, wc_ref, wv_ref, wz_ref, wgc_ref, wgp_ref, cw_ref, cst_ref,
                     zst_ref, bc_ref, pooled_ref, sgc_ref, sgp_ref, cvall_ref, zall_ref,
                     wb_c, wc_c, wv_c, wz_c, wgc_c, wgp_c, *, n_seq, n_j):
    seq = F32_SUBLANES
    ns = n_seq * seq
    j = pl.program_id(0)
    g = j // (n_j // N_GROUPS)
    u = u_ref[...]
    us = u[:ns]
    tc = cvall_ref.shape[1]
    tz = zall_ref.shape[1]

    cv_all = _dot(u, _weight(wc_ref, wc_c)) * _dot(u, _weight(wv_ref, wv_c))
    cvall_ref[...] = cv_all
    ext = jnp.concatenate([cst_ref[...], cv_all[:ns].reshape(n_seq, seq, tc)], axis=1)
    y = _conv3(cw_ref, ext.reshape(2 * ns, tc))
    y = y.reshape(n_seq, 2 * seq, tc)[:, seq:, :].reshape(ns, tc)
    bc_ref[...] = (_dot(us, _weight(wb_ref, wb_c)) * y).astype(BF16)

    z_all = _dot(u, _weight(wz_ref, wz_c))
    zall_ref[...] = z_all
    z = z_all[:ns]
    per = MAX_WIN + seq
    ext = jnp.concatenate([zst_ref[...], z.reshape(n_seq, seq, tz)], axis=1)
    sums = _window_sums(ext.reshape(n_seq * per, tz), g)
    sums = sums.reshape(n_seq, per, tz)[:, MAX_WIN:, :].reshape(ns, tz)
    pooled_ref[...] = (sums * _inv_window(g) - z).astype(BF16)

    sgc_ref[...] = jax.nn.sigmoid(_dot(us, _weight(wgc_ref, wgc_c)))
    sgp_ref[...] = jax.nn.sigmoid(_dot(us, _weight(wgp_ref, wgp_c)))


def _mix_sample(u, w_in, conv_w, cst, zst, *, n_j):
    rows, d = u.shape
    n_seq, _, d_pool = zst.shape
    ns = n_seq * F32_SUBLANES
    tc, tz = d // n_j, d_pool // n_j
    assert n_j % N_GROUPS == 0 and cst.shape == (n_seq, F32_SUBLANES, d)
    assert (3 * d + d_pool) % tc == 0
    conv_blocks = d // tc
    z0 = (3 * d) // tz
    gc0 = (3 * d + d_pool) // tc
    wcol = lambda w, off: pl.BlockSpec((d, w), lambda j: (0, off + j))
    w_specs = [wcol(tc, 0), wcol(tc, conv_blocks), wcol(tc, 2 * conv_blocks), wcol(tz, z0),
               wcol(tc, gc0), wcol(tc, gc0 + conv_blocks)]
    col = lambda r, w: pl.BlockSpec((r, w), lambda j: (0, j))
    in_specs = ([_resident((rows, d), lambda j: (0, 0))] + w_specs
                + [col(CONV_W, tc),
                   pl.BlockSpec((n_seq, F32_SUBLANES, tc), lambda j: (0, 0, j)),
                   pl.BlockSpec((n_seq, MAX_WIN, tz), lambda j: (0, 0, j))])
    group_widths = (d, d, d, d_pool, d, d)
    out_shape = ([jax.ShapeDtypeStruct((ns, d), BF16),
                  jax.ShapeDtypeStruct((ns, d_pool), BF16),
                  jax.ShapeDtypeStruct((ns, d), F32),
                  jax.ShapeDtypeStruct((ns, d), F32),
                  jax.ShapeDtypeStruct((rows, d), F32),
                  jax.ShapeDtypeStruct((rows, d_pool), F32)]
                 + [jax.ShapeDtypeStruct((d, w), BF16) for w in group_widths])
    out_specs = ([col(ns, tc), col(ns, tz), col(ns, tc), col(ns, tc), col(rows, tc), col(rows, tz)]
                 + [col(d, tz if w == d_pool else tc) for w in group_widths])
    return pl.pallas_call(
        functools.partial(_mix_sample_body, n_seq=n_seq, n_j=n_j),
        grid=(n_j,),
        in_specs=in_specs,
        out_specs=out_specs,
        out_shape=out_shape,
        compiler_params=_params(("arbitrary",)),
        name="mix_sample",
    )(u, w_in, w_in, w_in, w_in, w_in, w_in, conv_w, cst, zst)


def _outproj_body(bc_ref, pooled_ref, sgc_ref, sgp_ref, h_ref, wco_ref, wpg_ref, ps_ref, wo_ref,
                  h2_ref, *copies, emit_w):
    wco_c, wpg_c, wo_c = copies if emit_w else (None, None, None)
    n = pl.program_id(1)

    @pl.when(n == 0)
    def _():
        h2_ref[...] = h_ref[...]

    y_conv = _dot(bc_ref[...], _weight(wco_ref, wco_c))
    y_pool = _dot(pooled_ref[...], _weight(wpg_ref, wpg_c)[0]) * ps_ref[...]
    m = (sgc_ref[...] * y_conv + sgp_ref[...] * y_pool).astype(BF16)
    h2_ref[...] += _dot(m, _weight(wo_ref, wo_c))


def _outproj(bc, pooled, sgc, sgp, h, wco, wpg, pscale, wo, *, tm, tn, name):
    rows, d = bc.shape
    n_groups, pool_in, group_out = wpg.shape
    n_i, n_n = rows // tm, d // tn
    per_group = group_out // tn
    emit_w = wco.dtype == F32
    assert rows == n_i * tm and d == n_n * tn and group_out == per_group * tn
    assert n_i == 1 or not emit_w
    row_spec = _resident if n_i == 1 else pl.BlockSpec
    w_specs = [pl.BlockSpec((d, tn), lambda i, n: (0, n)),
               pl.BlockSpec((1, pool_in, tn), lambda i, n: (n // per_group, 0, n % per_group)),
               pl.BlockSpec((tn, d), lambda i, n: (n, 0))]
    in_specs = [
        row_spec((tm, d), lambda i, n: (i, 0)),
        pl.BlockSpec((tm, pool_in), lambda i, n: (i, n // per_group)),
        pl.BlockSpec((tm, tn), lambda i, n: (i, n)),
        pl.BlockSpec((tm, tn), lambda i, n: (i, n)),
        row_spec((tm, d), lambda i, n: (i, 0)),
        w_specs[0], w_specs[1],
        pl.BlockSpec((1, tn), lambda i, n: (0, n)),
        w_specs[2],
    ]
    w_shapes, w_out_specs = _weight_copies((wco, wpg, wo), w_specs, emit_w)
    return pl.pallas_call(
        functools.partial(_outproj_body, emit_w=emit_w),
        grid=(n_i, n_n),
        in_specs=in_specs,
        out_specs=[row_spec((tm, d), lambda i, n: (i, 0))] + w_out_specs,
        out_shape=[jax.ShapeDtypeStruct((rows, d), F32)] + w_shapes,
        compiler_params=_params(("arbitrary", "arbitrary")),
        name=name,
    )(bc, pooled, sgc, sgp, h, wco, wpg, pscale, wo)


def kernel(x_prompt, x_sample, state_conv, state_pool, meta_tokens, norm_ffn1, w_ffn1_gate,
           w_ffn1_up, w_ffn1_down, norm_mix, w_in, conv_w, w_conv_out, w_pool_group, pool_scale,
           w_o, norm_ffn2, w_ffn2_gate, w_ffn2_up, w_ffn2_down, norm_final):
    batch, seq, d = x_prompt.shape
    dec_batch, dec_seq, _ = x_sample.shape
    depth = w_in.shape[0]
    d_pool = state_pool.shape[-1]
    assert depth == 1 and dec_seq == F32_SUBLANES and meta_tokens.shape[0] == N_META == MAX_WIN
    assert state_conv.shape[2] == CONV_W - 1 and state_pool.shape[2] == MAX_WIN - 1
    n_s = dec_batch * dec_seq

    row = lambda v: v.reshape(1, d)
    g1, gm, g2, gf = row(norm_ffn1[0]), row(norm_mix[0]), row(norm_ffn2[0]), row(norm_final)
    pscale = row(pool_scale[0])
    cw = conv_w[0]

    xs = jnp.concatenate([x_sample.reshape(n_s, d), meta_tokens.astype(F32)], axis=0)
    h_s, u_s, wg1, wu1, wd1 = _ffn(xs, g1, w_ffn1_gate[0], w_ffn1_up[0], w_ffn1_down[0], gm,
                                   tm=xs.shape[0], tf=256, emit_h=True, name="ffn1_sample")
    cst = jnp.pad(state_conv[0], ((0, 0), (F32_SUBLANES - (CONV_W - 1), 0), (0, 0)))
    zst = jnp.pad(state_pool[0], ((0, 0), (1, 0), (0, 0)))
    bc_s, pooled_s, sgc_s, sgp_s, cv_all, z_all, *w_groups = _mix_sample(u_s, w_in[0], cw, cst, zst, n_j=8)
    h2_s, wco, wpg, wo = _outproj(bc_s, pooled_s, sgc_s, sgp_s, h_s, w_conv_out[0], w_pool_group[0],
                                  pscale, w_o[0], tm=n_s, tn=256, name="outproj_sample")
    y_s, wg2, wu2, wd2 = _ffn(h2_s, g2, w_ffn2_gate[0], w_ffn2_up[0], w_ffn2_down[0], gf,
                              tm=n_s, tf=256, emit_h=False, name="ffn2_sample")

    cv_s = cv_all[:n_s].reshape(dec_batch, dec_seq, d)
    new_conv_sample = cv_s[:, dec_seq - (CONV_W - 1):]
    new_pool_sample = jnp.concatenate(
        [state_pool[0][:, dec_seq:], z_all[:n_s].reshape(dec_batch, dec_seq, d_pool)], axis=1)

    cinit = jnp.pad(cv_all[n_s + N_META - (CONV_W - 1):], ((F32_SUBLANES - (CONV_W - 1), 0), (0, 0)))
    zinit = z_all[n_s:]
    xp = x_prompt.reshape(batch * seq, d)
    h_p, u_p = _ffn(xp, g1, wg1, wu1, wd1, gm, tm=512, tf=512, emit_h=True, name="ffn1_prompt")
    bc_p, pooled_p, sgc_p, sgp_p, cvlast, zlast = _mix_prompt(u_p, w_groups, cw, cinit, zinit,
                                                              n_seq=batch, tm=512, n_j=4)
    (h2_p,) = _outproj(bc_p, pooled_p, sgc_p, sgp_p, h_p, wco, wpg, pscale, wo, tm=512, tn=512,
                       name="outproj_prompt")
    (y_p,) = _ffn(h2_p, g2, wg2, wu2, wd2, gf, tm=512, tf=512, emit_h=False, name="ffn2_prompt")

    return (y_p.reshape(batch, seq, d),
            y_s.reshape(dec_batch, dec_seq, d),
            cvlast[None],
            zlast[:, 1:][None],
            new_conv_sample[None],
            new_pool_sample[None])
```

```python
import functools

import jax
import jax.numpy as jnp
from jax import lax
from jax.experimental import pallas as pl
from jax.experimental.pallas import tpu as pltpu

F32 = jnp.float32
BF16 = jnp.bfloat16

EPS = 1e-6
N_META = 16
CONV_W = 3
POOL_WINDOWS = (2, 4, 8, 16)
MAX_WIN = max(POOL_WINDOWS)
N_GROUPS = len(POOL_WINDOWS)

F32_SUBLANES = 8
V7X_VMEM_LIMIT_BYTES = 60 * 1024 * 1024


def _params(semantics):
    return pltpu.CompilerParams(dimension_semantics=semantics,
                                vmem_limit_bytes=V7X_VMEM_LIMIT_BYTES)


def _dot(a, b):
    return jnp.dot(a, b, preferred_element_type=F32)


def _rms(x, g):
    ms = jnp.mean(x * x, axis=-1, keepdims=True)
    return (x * lax.rsqrt(ms + EPS)) * g


def _resident(shape, index_map):
    return pl.BlockSpec(shape, index_map, pipeline_mode=pl.Buffered(1))


def _weight(w_ref, copy_ref):
    w = w_ref[...].astype(BF16)
    if copy_ref is not None:
        copy_ref[...] = w
    return w


def _weight_copies(arrays, specs, emit):
    if not emit:
        return [], []
    return [jax.ShapeDtypeStruct(a.shape, BF16) for a in arrays], list(specs)


def _ffn_body(x_ref, g_ref, wg_ref, wu_ref, wd_ref, g2_ref, *refs, emit_h, emit_w):
    refs = list(refs)
    acc_ref = refs.pop(0)
    n_ref = refs.pop(0) if emit_h else acc_ref
    copies = [refs.pop(0) for _ in range(3)] if emit_w else [None] * 3
    (xn_ref,) = refs
    f = pl.program_id(1)

    @pl.when(f == 0)
    def _():
        xn_ref[...] = _rms(x_ref[...], g_ref[...]).astype(BF16)
        acc_ref[...] = jnp.zeros(acc_ref.shape, F32)

    xn = xn_ref[...]
    a = _dot(xn, _weight(wg_ref, copies[0]))
    b = _dot(xn, _weight(wu_ref, copies[1]))
    act = ((a * jax.nn.sigmoid(a)) * b).astype(BF16)
    acc_ref[...] += _dot(act, _weight(wd_ref, copies[2]))

    @pl.when(f == pl.num_programs(1) - 1)
    def _():
        h = x_ref[...] + 0.5 * acc_ref[...]
        if emit_h:
            acc_ref[...] = h
        n_ref[...] = _rms(h, g2_ref[...]).astype(n_ref.dtype)


def _ffn(x, g, wg, wu, wd, g2, *, tm, tf, emit_h, name):
    rows, d = x.shape
    d_ff = wg.shape[1]
    n_i, n_f = rows // tm, d_ff // tf
    emit_w = wg.dtype == F32
    assert rows == n_i * tm and d_ff == n_f * tf
    assert n_i == 1 or not emit_w
    row_spec = _resident if n_i == 1 else pl.BlockSpec
    row = lambda i, f: (i, 0)
    w_specs = [pl.BlockSpec((d, tf), lambda i, f: (0, f)),
               pl.BlockSpec((d, tf), lambda i, f: (0, f)),
               pl.BlockSpec((tf, d), lambda i, f: (f, 0))]
    in_specs = ([row_spec((tm, d), row), pl.BlockSpec((1, d), lambda i, f: (0, 0))] + w_specs
                + [pl.BlockSpec((1, d), lambda i, f: (0, 0))])
    out_shape = [jax.ShapeDtypeStruct((rows, d), F32)]
    out_specs = [row_spec((tm, d), row)]
    if emit_h:
        out_shape.append(jax.ShapeDtypeStruct((rows, d), BF16))
        out_specs.append(row_spec((tm, d), row))
    w_shapes, w_out_specs = _weight_copies((wg, wu, wd), w_specs, emit_w)
    return pl.pallas_call(
        functools.partial(_ffn_body, emit_h=emit_h, emit_w=emit_w),
        grid=(n_i, n_f),
        in_specs=in_specs,
        out_specs=out_specs + w_out_specs,
        out_shape=out_shape + w_shapes,
        scratch_shapes=[pltpu.VMEM((tm, d), BF16)],
        compiler_params=_params(("arbitrary", "arbitrary")),
        name=name,
    )(x, g, wg, wu, wd, g2)


def _conv3(w_ref, ext):
    w = w_ref[...]
    y = w[0:1] * pltpu.roll(ext, 2, axis=0)
    y = y + w[1:2] * pltpu.roll(ext, 1, axis=0)
    return y + w[2:3] * ext


def _window_sums(e, g):
    p = e + pltpu.roll(e, 1, axis=0)
    for level, shift in ((1, 2), (2, 4), (3, 8)):
        p = jnp.where(g >= level, p + pltpu.roll(p, shift, axis=0), p)
    return p


def _inv_window(g):
    return jnp.where(g == 0, 0.5, jnp.where(g == 1, 0.25, jnp.where(g == 2, 0.125, 0.0625))).astype(F32)


def _mix_prompt_body(u_ref, wb_ref, wc_ref, wv_ref, wz_ref, wgc_ref, wgp_ref, cw_ref, cinit_ref,
                     zinit_ref, bc_ref, pooled_ref, sgc_ref, sgp_ref, cvlast_ref, zlast_ref,
                     cpast_ref, zpast_ref, *, tm, tiles_per_seq, n_j):
    j = pl.program_id(0)
    i = pl.program_id(1)
    t = i % tiles_per_seq
    b = i // tiles_per_seq
    g = j // (n_j // N_GROUPS)
    u = u_ref[...]

    @pl.when(t == 0)
    def _():
        cpast_ref[...] = cinit_ref[...]
        zpast_ref[...] = zinit_ref[...]

    cv = _dot(u, wc_ref[...]) * _dot(u, wv_ref[...])
    ext = jnp.concatenate([cpast_ref[...], cv], axis=0)
    y = _conv3(cw_ref, ext)[F32_SUBLANES:]
    bc_ref[...] = (_dot(u, wb_ref[...]) * y).astype(BF16)
    cpast_ref[...] = cv[tm - F32_SUBLANES:]

    z = _dot(u, wz_ref[...])
    sums = _window_sums(jnp.concatenate([zpast_ref[...], z], axis=0), g)[MAX_WIN:]
    pooled_ref[...] = (sums * _inv_window(g) - z).astype(BF16)
    zpast_ref[...] = z[tm - MAX_WIN:]

    @pl.when(t == tiles_per_seq - 1)
    def _():
        cvlast_ref[b] = cpast_ref[pl.ds(F32_SUBLANES - (CONV_W - 1), CONV_W - 1), :]
        zlast_ref[b] = zpast_ref[...]

    sgc_ref[...] = jax.nn.sigmoid(_dot(u, wgc_ref[...]))
    sgp_ref[...] = jax.nn.sigmoid(_dot(u, wgp_ref[...]))


def _mix_prompt(u, w_groups, conv_w, cinit, zinit, *, n_seq, tm, n_j):
    rows, d = u.shape
    d_pool = zinit.shape[1]
    tc, tz = d // n_j, d_pool // n_j
    seq = rows // n_seq
    tiles_per_seq = seq // tm
    n_i = rows // tm
    assert seq == tiles_per_seq * tm and n_j % N_GROUPS == 0 and tm >= MAX_WIN
    blk = lambda w: pl.BlockSpec((tm, w), lambda j, i: (i, j))
    wcol = lambda w: pl.BlockSpec((d, w), lambda j, i: (0, j))
    in_specs = ([pl.BlockSpec((tm, d), lambda j, i: (i, 0))]
                + [wcol(tc), wcol(tc), wcol(tc), wcol(tz), wcol(tc), wcol(tc)]
                + [pl.BlockSpec((CONV_W, tc), lambda j, i: (0, j)),
                   pl.BlockSpec((F32_SUBLANES, tc), lambda j, i: (0, j)),
                   pl.BlockSpec((MAX_WIN, tz), lambda j, i: (0, j))])
    out_shape = (jax.ShapeDtypeStruct((rows, d), BF16),
                 jax.ShapeDtypeStruct((rows, d_pool), BF16),
                 jax.ShapeDtypeStruct((rows, d), F32),
                 jax.ShapeDtypeStruct((rows, d), F32),
                 jax.ShapeDtypeStruct((n_seq, CONV_W - 1, d), F32),
                 jax.ShapeDtypeStruct((n_seq, MAX_WIN, d_pool), F32))
    out_specs = (blk(tc), blk(tz), blk(tc), blk(tc),
                 pl.BlockSpec((n_seq, CONV_W - 1, tc), lambda j, i: (0, 0, j)),
                 pl.BlockSpec((n_seq, MAX_WIN, tz), lambda j, i: (0, 0, j)))
    scratch = [pltpu.VMEM((F32_SUBLANES, tc), F32), pltpu.VMEM((MAX_WIN, tz), F32)]
    return pl.pallas_call(
        functools.partial(_mix_prompt_body, tm=tm, tiles_per_seq=tiles_per_seq, n_j=n_j),
        grid=(n_j, n_i),
        in_specs=in_specs,
        out_specs=out_specs,
        out_shape=out_shape,
        scratch_shapes=scratch,
        compiler_params=_params(("arbitrary", "arbitrary")),
        name="mix_prompt",
    )(u, *w_groups, conv_w, cinit, zinit)


def _mix_sample_body(u_ref, wb_ref, wc_ref, wv_ref, wz_ref, wgc_ref, wgp_ref, cw_ref, cst_ref,
                     zst_ref, bc_ref, pooled_ref, sgc_ref, sgp_ref, cvall_ref, zall_ref,
                     wb_c, wc_c, wv_c, wz_c, wgc_c, wgp_c, *, n_seq, n_j):
    seq = F32_SUBLANES
    ns = n_seq * seq
    j = pl.program_id(0)
    g = j // (n_j // N_GROUPS)
    u = u_ref[...]
    us = u[:ns]
    tc = cvall_ref.shape[1]
    tz = zall_ref.shape[1]

    cv_all = _dot(u, _weight(wc_ref, wc_c)) * _dot(u, _weight(wv_ref, wv_c))
    cvall_ref[...] = cv_all
    ext = jnp.concatenate([cst_ref[...], cv_all[:ns].reshape(n_seq, seq, tc)], axis=1)
    y = _conv3(cw_ref, ext.reshape(2 * ns, tc))
    y = y.reshape(n_seq, 2 * seq, tc)[:, seq:, :].reshape(ns, tc)
    bc_ref[...] = (_dot(us, _weight(wb_ref, wb_c)) * y).astype(BF16)

    z_all = _dot(u, _weight(wz_ref, wz_c))
    zall_ref[...] = z_all
    z = z_all[:ns]
    per = MAX_WIN + seq
    ext = jnp.concatenate([zst_ref[...], z.reshape(n_seq, seq, tz)], axis=1)
    sums = _window_sums(ext.reshape(n_seq * per, tz), g)
    sums = sums.reshape(n_seq, per, tz)[:, MAX_WIN:, :].reshape(ns, tz)
    pooled_ref[...] = (sums * _inv_window(g) - z).astype(BF16)

    sgc_ref[...] = jax.nn.sigmoid(_dot(us, _weight(wgc_ref, wgc_c)))
    sgp_ref[...] = jax.nn.sigmoid(_dot(us, _weight(wgp_ref, wgp_c)))


def _mix_sample(u, w_in, conv_w, cst, zst, *, n_j):
    rows, d = u.shape
    n_seq, _, d_pool = zst.shape
    ns = n_seq * F32_SUBLANES
    tc, tz = d // n_j, d_pool // n_j
    assert n_j % N_GROUPS == 0 and cst.shape == (n_seq, F32_SUBLANES, d)
    assert (3 * d + d_pool) % tc == 0
    conv_blocks = d // tc
    z0 = (3 * d) // tz
    gc0 = (3 * d + d_pool) // tc
    wcol = lambda w, off: pl.BlockSpec((d, w), lambda j: (0, off + j))
    w_specs = [wcol(tc, 0), wcol(tc, conv_blocks), wcol(tc, 2 * conv_blocks), wcol(tz, z0),
               wcol(tc, gc0), wcol(tc, gc0 + conv_blocks)]
    col = lambda r, w: pl.BlockSpec((r, w), lambda j: (0, j))
    in_specs = ([_resident((rows, d), lambda j: (0, 0))] + w_specs
                + [col(CONV_W, tc),
                   pl.BlockSpec((n_seq, F32_SUBLANES, tc), lambda j: (0, 0, j)),
                   pl.BlockSpec((n_seq, MAX_WIN, tz), lambda j: (0, 0, j))])
    group_widths = (d, d, d, d_pool, d, d)
    out_shape = ([jax.ShapeDtypeStruct((ns, d), BF16),
                  jax.ShapeDtypeStruct((ns, d_pool), BF16),
                  jax.ShapeDtypeStruct((ns, d), F32),
                  jax.ShapeDtypeStruct((ns, d), F32),
                  jax.ShapeDtypeStruct((rows, d), F32),
                  jax.ShapeDtypeStruct((rows, d_pool), F32)]
                 + [jax.ShapeDtypeStruct((d, w), BF16) for w in group_widths])
    out_specs = ([col(ns, tc), col(ns, tz), col(ns, tc), col(ns, tc), col(rows, tc), col(rows, tz)]
                 + [col(d, tz if w == d_pool else tc) for w in group_widths])
    return pl.pallas_call(
        functools.partial(_mix_sample_body, n_seq=n_seq, n_j=n_j),
        grid=(n_j,),
        in_specs=in_specs,
        out_specs=out_specs,
        out_shape=out_shape,
        compiler_params=_params(("arbitrary",)),
        name="mix_sample",
    )(u, w_in, w_in, w_in, w_in, w_in, w_in, conv_w, cst, zst)


def _merged_block(bc_ref, pooled_ref, sgc_ref, sgp_ref, ps_ref, wco, wpg):
    y_conv = _dot(bc_ref[...], wco)
    y_pool = _dot(pooled_ref[...], wpg) * ps_ref[...]
    return (sgc_ref[...] * y_conv + sgp_ref[...] * y_pool).astype(BF16)


def _outproj_body(bc_ref, pooled_ref, sgc_ref, sgp_ref, h_ref, wco_ref, wpg_ref, ps_ref, wo_ref,
                  h2_ref, *copies, emit_w, resident_w):
    wco_c, wpg_c, wo_c = copies if emit_w else (None, None, None)
    n = pl.program_id(1)

    @pl.when(n == 0)
    def _():
        h2_ref[...] = h_ref[...]

    if resident_w:
        wco, wpg, wo = wco_ref[n], wpg_ref[n], wo_ref[n]
    else:
        wco, wpg, wo = _weight(wco_ref, wco_c), _weight(wpg_ref, wpg_c)[0], _weight(wo_ref, wo_c)
    m = _merged_block(bc_ref, pooled_ref, sgc_ref, sgp_ref, ps_ref, wco, wpg)
    h2_ref[...] += _dot(m, wo)


def _outproj(bc, pooled, sgc, sgp, h, wco, wpg, pscale, wo, *, tm, tn, name):
    rows, d = bc.shape
    n_groups, pool_in, group_out = wpg.shape
    n_i, n_n = rows // tm, d // tn
    per_group = group_out // tn
    emit_w = wco.dtype == F32
    resident_w = not emit_w
    assert rows == n_i * tm and d == n_n * tn and group_out == per_group * tn
    row_spec = _resident if n_i == 1 else pl.BlockSpec
    if resident_w:
        assert per_group == 1 and wco.shape == (n_n, d, tn) and wo.shape == (n_n, tn, d)
        whole = lambda a: _resident(a.shape, lambda i, n: (0, 0, 0))
        w_specs = [whole(wco), whole(wpg), whole(wo)]
    else:
        assert n_i == 1
        w_specs = [pl.BlockSpec((d, tn), lambda i, n: (0, n)),
                   pl.BlockSpec((1, pool_in, tn), lambda i, n: (n // per_group, 0, n % per_group)),
                   pl.BlockSpec((tn, d), lambda i, n: (n, 0))]
    in_specs = [
        row_spec((tm, d), lambda i, n: (i, 0)),
        pl.BlockSpec((tm, pool_in), lambda i, n: (i, n // per_group)),
        pl.BlockSpec((tm, tn), lambda i, n: (i, n)),
        pl.BlockSpec((tm, tn), lambda i, n: (i, n)),
        row_spec((tm, d), lambda i, n: (i, 0)),
        w_specs[0], w_specs[1],
        pl.BlockSpec((1, tn), lambda i, n: (0, n)),
        w_specs[2],
    ]
    w_shapes, w_out_specs = _weight_copies((wco, wpg, wo), w_specs, emit_w)
    return pl.pallas_call(
        functools.partial(_outproj_body, emit_w=emit_w, resident_w=resident_w),
        grid=(n_i, n_n),
        in_specs=in_specs,
        out_specs=[row_spec((tm, d), lambda i, n: (i, 0))] + w_out_specs,
        out_shape=[jax.ShapeDtypeStruct((rows, d), F32)] + w_shapes,
        compiler_params=_params(("arbitrary", "arbitrary")),
        name=name,
    )(bc, pooled, sgc, sgp, h, wco, wpg, pscale, wo)


def kernel(x_prompt, x_sample, state_conv, state_pool, meta_tokens, norm_ffn1, w_ffn1_gate,
           w_ffn1_up, w_ffn1_down, norm_mix, w_in, conv_w, w_conv_out, w_pool_group, pool_scale,
           w_o, norm_ffn2, w_ffn2_gate, w_ffn2_up, w_ffn2_down, norm_final):
    batch, seq, d = x_prompt.shape
    dec_batch, dec_seq, _ = x_sample.shape
    depth = w_in.shape[0]
    d_pool = state_pool.shape[-1]
    assert depth == 1 and dec_seq == F32_SUBLANES and meta_tokens.shape[0] == N_META == MAX_WIN
    assert state_conv.shape[2] == CONV_W - 1 and state_pool.shape[2] == MAX_WIN - 1
    n_s = dec_batch * dec_seq

    row = lambda v: v.reshape(1, d)
    g1, gm, g2, gf = row(norm_ffn1[0]), row(norm_mix[0]), row(norm_ffn2[0]), row(norm_final)
    pscale = row(pool_scale[0])
    cw = conv_w[0]

    xs = jnp.concatenate([x_sample.reshape(n_s, d), meta_tokens.astype(F32)], axis=0)
    h_s, u_s, wg1, wu1, wd1 = _ffn(xs, g1, w_ffn1_gate[0], w_ffn1_up[0], w_ffn1_down[0], gm,
                                   tm=xs.shape[0], tf=256, emit_h=True, name="ffn1_sample")
    cst = jnp.pad(state_conv[0], ((0, 0), (F32_SUBLANES - (CONV_W - 1), 0), (0, 0)))
    zst = jnp.pad(state_pool[0], ((0, 0), (1, 0), (0, 0)))
    bc_s, pooled_s, sgc_s, sgp_s, cv_all, z_all, *w_groups = _mix_sample(u_s, w_in[0], cw, cst, zst, n_j=8)
    h2_s, wco, wpg, wo = _outproj(bc_s, pooled_s, sgc_s, sgp_s, h_s, w_conv_out[0], w_pool_group[0],
                                  pscale, w_o[0], tm=n_s, tn=256, name="outproj_sample")
    y_s, wg2, wu2, wd2 = _ffn(h2_s, g2, w_ffn2_gate[0], w_ffn2_up[0], w_ffn2_down[0], gf,
                              tm=n_s, tf=256, emit_h=False, name="ffn2_sample")

    cv_s = cv_all[:n_s].reshape(dec_batch, dec_seq, d)
    new_conv_sample = cv_s[:, dec_seq - (CONV_W - 1):]
    new_pool_sample = jnp.concatenate(
        [state_pool[0][:, dec_seq:], z_all[:n_s].reshape(dec_batch, dec_seq, d_pool)], axis=1)

    cinit = jnp.pad(cv_all[n_s + N_META - (CONV_W - 1):], ((F32_SUBLANES - (CONV_W - 1), 0), (0, 0)))
    zinit = z_all[n_s:]
    xp = x_prompt.reshape(batch * seq, d)
    h_p, u_p = _ffn(xp, g1, wg1, wu1, wd1, gm, tm=512, tf=512, emit_h=True, name="ffn1_prompt")
    bc_p, pooled_p, sgc_p, sgp_p, cvlast, zlast = _mix_prompt(u_p, w_groups, cw, cinit, zinit,
                                                              n_seq=batch, tm=1024, n_j=4)
    tn = wpg.shape[2]
    wco_chunks = wco.reshape(d, d // tn, tn).transpose(1, 0, 2)
    wo_chunks = wo.reshape(d // tn, tn, d)
    (h2_p,) = _outproj(bc_p, pooled_p, sgc_p, sgp_p, h_p, wco_chunks, wpg, pscale, wo_chunks, tm=512, tn=tn,
                       name="outproj_prompt")
    (y_p,) = _ffn(h2_p, g2, wg2, wu2, wd2, gf, tm=512, tf=512, emit_h=False, name="ffn2_prompt")

    return (y_p.reshape(batch, seq, d),
            y_s.reshape(dec_batch, dec_seq, d),
            cvlast[None],
            zlast[:, 1:][None],
            new_conv_sample[None],
            new_pool_sample[None])
```

```python
import functools

import jax
import jax.numpy as jnp
from jax import lax
from jax.experimental import pallas as pl
from jax.experimental.pallas import tpu as pltpu

F32 = jnp.float32
BF16 = jnp.bfloat16

EPS = 1e-6
N_META = 16
CONV_W = 3
POOL_WINDOWS = (2, 4, 8, 16)
MAX_WIN = max(POOL_WINDOWS)
N_GROUPS = len(POOL_WINDOWS)

F32_SUBLANES = 8
V7X_VMEM_LIMIT_BYTES = 60 * 1024 * 1024


def _params(semantics):
    return pltpu.CompilerParams(dimension_semantics=semantics,
                                vmem_limit_bytes=V7X_VMEM_LIMIT_BYTES)


def _dot(a, b):
    return jnp.dot(a, b, preferred_element_type=F32)


def _rms(x, g):
    ms = jnp.mean(x * x, axis=-1, keepdims=True)
    return (x * lax.rsqrt(ms + EPS)) * g


def _resident(shape, index_map):
    return pl.BlockSpec(shape, index_map, pipeline_mode=pl.Buffered(1))


def _block(ref):
    return ref[0] if len(ref.shape) == 3 else ref[...]


def _set_block(ref, value):
    if len(ref.shape) == 3:
        ref[0] = value
    else:
        ref[...] = value


def _weight(w_ref, copy_ref):
    w = _block(w_ref).astype(BF16)
    if copy_ref is not None:
        _set_block(copy_ref, w)
    return w


def _ffn_body(x_ref, g_ref, wg_ref, wu_ref, wd_ref, g2_ref, *refs, emit_h, emit_w):
    refs = list(refs)
    acc_ref = refs.pop(0)
    n_ref = refs.pop(0) if emit_h else acc_ref
    copies = [refs.pop(0) for _ in range(3)] if emit_w else [None] * 3
    (xn_ref,) = refs
    f = pl.program_id(1)

    @pl.when(f == 0)
    def _():
        xn_ref[...] = _rms(x_ref[...], g_ref[...]).astype(BF16)
        acc_ref[...] = jnp.zeros(acc_ref.shape, F32)

    xn = xn_ref[...]
    a = _dot(xn, _weight(wg_ref, copies[0]))
    b = _dot(xn, _weight(wu_ref, copies[1]))
    act = ((a * jax.nn.sigmoid(a)) * b).astype(BF16)
    acc_ref[...] += _dot(act, _weight(wd_ref, copies[2]))

    @pl.when(f == pl.num_programs(1) - 1)
    def _():
        h = x_ref[...] + 0.5 * acc_ref[...]
        if emit_h:
            acc_ref[...] = h
        n_ref[...] = _rms(h, g2_ref[...]).astype(n_ref.dtype)


def _ffn(x, g, wg, wu, wd, g2, *, tm, tf, emit_h, name, w_chunk=None):
    rows, d = x.shape
    d_ff = wd.shape[0]
    n_i, n_f = rows // tm, d_ff // tf
    emit_w = wg.dtype == F32
    assert rows == n_i * tm and d_ff == n_f * tf
    row_spec = _resident if n_i == 1 else pl.BlockSpec
    row = lambda i, f: (i, 0)
    wd_spec = pl.BlockSpec((tf, d), lambda i, f: (f, 0))
    if emit_w:
        assert n_i == 1 and w_chunk % tf == 0
        per_chunk = w_chunk // tf
        col_spec = pl.BlockSpec((d, tf), lambda i, f: (0, f))
        w_specs = [col_spec, col_spec, wd_spec]
        chunk_spec = pl.BlockSpec((1, d, tf), lambda i, f: (f // per_chunk, 0, f % per_chunk))
        chunked = jax.ShapeDtypeStruct((d_ff // w_chunk, d, w_chunk), BF16)
        w_shapes = [chunked, chunked, jax.ShapeDtypeStruct(wd.shape, BF16)]
        w_out_specs = [chunk_spec, chunk_spec, wd_spec]
    else:
        assert wg.shape == wu.shape == (n_f, d, tf)
        chunk_spec = pl.BlockSpec((1, d, tf), lambda i, f: (f, 0, 0))
        w_specs = [chunk_spec, chunk_spec, wd_spec]
        w_shapes, w_out_specs = [], []
    in_specs = ([row_spec((tm, d), row), pl.BlockSpec((1, d), lambda i, f: (0, 0))] + w_specs
                + [pl.BlockSpec((1, d), lambda i, f: (0, 0))])
    out_shape = [jax.ShapeDtypeStruct((rows, d), F32)]
    out_specs = [row_spec((tm, d), row)]
    if emit_h:
        out_shape.append(jax.ShapeDtypeStruct((rows, d), BF16))
        out_specs.append(row_spec((tm, d), row))
    return pl.pallas_call(
        functools.partial(_ffn_body, emit_h=emit_h, emit_w=emit_w),
        grid=(n_i, n_f),
        in_specs=in_specs,
        out_specs=out_specs + w_out_specs,
        out_shape=out_shape + w_shapes,
        scratch_shapes=[pltpu.VMEM((tm, d), BF16)],
        compiler_params=_params(("arbitrary", "arbitrary")),
        name=name,
    )(x, g, wg, wu, wd, g2)


def _conv3(w_ref, ext):
    w = w_ref[...]
    y = w[0:1] * pltpu.roll(ext, 2, axis=0)
    y = y + w[1:2] * pltpu.roll(ext, 1, axis=0)
    return y + w[2:3] * ext


def _window_sums(e, g):
    p = e + pltpu.roll(e, 1, axis=0)
    for level, shift in ((1, 2), (2, 4), (3, 8)):
        p = jnp.where(g >= level, p + pltpu.roll(p, shift, axis=0), p)
    return p


def _inv_window(g):
    return jnp.where(g == 0, 0.5, jnp.where(g == 1, 0.25, jnp.where(g == 2, 0.125, 0.0625))).astype(F32)


def _mix_prompt_body(u_ref, wb_ref, wc_ref, wv_ref, wz_ref, wgc_ref, wgp_ref, cw_ref, cinit_ref,
                     zinit_ref, bc_ref, pooled_ref, sgc_ref, sgp_ref, cvlast_ref, zlast_ref,
                     cpast_ref, zpast_ref, *, tm, tiles_per_seq, n_j):
    j = pl.program_id(0)
    i = pl.program_id(1)
    t = i % tiles_per_seq
    b = i // tiles_per_seq
    g = j // (n_j // N_GROUPS)
    u = u_ref[...]

    @pl.when(t == 0)
    def _():
        cpast_ref[...] = cinit_ref[...]
        zpast_ref[...] = zinit_ref[...]

    cv = _dot(u, wc_ref[...]) * _dot(u, wv_ref[...])
    ext = jnp.concatenate([cpast_ref[...], cv], axis=0)
    y = _conv3(cw_ref, ext)[F32_SUBLANES:]
    bc_ref[0] = (_dot(u, wb_ref[...]) * y).astype(BF16)
    cpast_ref[...] = cv[tm - F32_SUBLANES:]

    z = _dot(u, wz_ref[...])
    sums = _window_sums(jnp.concatenate([zpast_ref[...], z], axis=0), g)[MAX_WIN:]
    pooled_ref[0] = (sums * _inv_window(g) - z).astype(BF16)
    zpast_ref[...] = z[tm - MAX_WIN:]

    @pl.when(t == tiles_per_seq - 1)
    def _():
        cvlast_ref[b] = cpast_ref[pl.ds(F32_SUBLANES - (CONV_W - 1), CONV_W - 1), :]
        zlast_ref[b] = zpast_ref[...]

    sgc_ref[0] = jax.nn.sigmoid(_dot(u, wgc_ref[...]))
    sgp_ref[0] = jax.nn.sigmoid(_dot(u, wgp_ref[...]))


def _mix_prompt(u, w_groups, conv_w, cinit, zinit, *, n_seq, tm, n_j):
    rows, d = u.shape
    d_pool = zinit.shape[1]
    tc, tz = d // n_j, d_pool // n_j
    seq = rows // n_seq
    tiles_per_seq = seq // tm
    n_i = rows // tm
    assert seq == tiles_per_seq * tm and n_j % N_GROUPS == 0 and tm >= MAX_WIN
    blk = lambda w: pl.BlockSpec((1, tm, w), lambda j, i: (j, i, 0))
    wcol = lambda w: pl.BlockSpec((d, w), lambda j, i: (0, j))
    in_specs = ([pl.BlockSpec((tm, d), lambda j, i: (i, 0))]
                + [wcol(tc), wcol(tc), wcol(tc), wcol(tz), wcol(tc), wcol(tc)]
                + [pl.BlockSpec((CONV_W, tc), lambda j, i: (0, j)),
                   pl.BlockSpec((F32_SUBLANES, tc), lambda j, i: (0, j)),
                   pl.BlockSpec((MAX_WIN, tz), lambda j, i: (0, j))])
    out_shape = (jax.ShapeDtypeStruct((n_j, rows, tc), BF16),
                 jax.ShapeDtypeStruct((n_j, rows, tz), BF16),
                 jax.ShapeDtypeStruct((n_j, rows, tc), F32),
                 jax.ShapeDtypeStruct((n_j, rows, tc), F32),
                 jax.ShapeDtypeStruct((n_seq, CONV_W - 1, d), F32),
                 jax.ShapeDtypeStruct((n_seq, MAX_WIN, d_pool), F32))
    out_specs = (blk(tc), blk(tz), blk(tc), blk(tc),
                 pl.BlockSpec((n_seq, CONV_W - 1, tc), lambda j, i: (0, 0, j)),
                 pl.BlockSpec((n_seq, MAX_WIN, tz), lambda j, i: (0, 0, j)))
    scratch = [pltpu.VMEM((F32_SUBLANES, tc), F32), pltpu.VMEM((MAX_WIN, tz), F32)]
    return pl.pallas_call(
        functools.partial(_mix_prompt_body, tm=tm, tiles_per_seq=tiles_per_seq, n_j=n_j),
        grid=(n_j, n_i),
        in_specs=in_specs,
        out_specs=out_specs,
        out_shape=out_shape,
        scratch_shapes=scratch,
        compiler_params=_params(("arbitrary", "arbitrary")),
        name="mix_prompt",
    )(u, *w_groups, conv_w, cinit, zinit)


def _mix_sample_body(u_ref, wb_ref, wc_ref, wv_ref, wz_ref, wgc_ref, wgp_ref, cw_ref, cst_ref,
                     zst_ref, bc_ref, pooled_ref, sgc_ref, sgp_ref, cvall_ref, zall_ref,
                     wb_c, wc_c, wv_c, wz_c, wgc_c, wgp_c, *, n_seq, n_j):
    seq = F32_SUBLANES
    ns = n_seq * seq
    j = pl.program_id(0)
    g = j // (n_j // N_GROUPS)
    u = u_ref[...]
    us = u[:ns]
    tc = cvall_ref.shape[1]
    tz = zall_ref.shape[1]

    cv_all = _dot(u, _weight(wc_ref, wc_c)) * _dot(u, _weight(wv_ref, wv_c))
    cvall_ref[...] = cv_all
    ext = jnp.concatenate([cst_ref[...], cv_all[:ns].reshape(n_seq, seq, tc)], axis=1)
    y = _conv3(cw_ref, ext.reshape(2 * ns, tc))
    y = y.reshape(n_seq, 2 * seq, tc)[:, seq:, :].reshape(ns, tc)
    bc_ref[0] = (_dot(us, _weight(wb_ref, wb_c)) * y).astype(BF16)

    z_all = _dot(u, _weight(wz_ref, wz_c))
    zall_ref[...] = z_all
    z = z_all[:ns]
    per = MAX_WIN + seq
    ext = jnp.concatenate([zst_ref[...], z.reshape(n_seq, seq, tz)], axis=1)
    sums = _window_sums(ext.reshape(n_seq * per, tz), g)
    sums = sums.reshape(n_seq, per, tz)[:, MAX_WIN:, :].reshape(ns, tz)
    pooled_ref[0] = (sums * _inv_window(g) - z).astype(BF16)

    sgc_ref[0] = jax.nn.sigmoid(_dot(us, _weight(wgc_ref, wgc_c)))
    sgp_ref[0] = jax.nn.sigmoid(_dot(us, _weight(wgp_ref, wgp_c)))


def _mix_sample(u, w_in, conv_w, cst, zst, *, n_j):
    rows, d = u.shape
    n_seq, _, d_pool = zst.shape
    ns = n_seq * F32_SUBLANES
    tc, tz = d // n_j, d_pool // n_j
    assert n_j % N_GROUPS == 0 and cst.shape == (n_seq, F32_SUBLANES, d)
    assert (3 * d + d_pool) % tc == 0
    conv_blocks = d // tc
    z0 = (3 * d) // tz
    gc0 = (3 * d + d_pool) // tc
    wcol = lambda w, off: pl.BlockSpec((d, w), lambda j: (0, off + j))
    w_specs = [wcol(tc, 0), wcol(tc, conv_blocks), wcol(tc, 2 * conv_blocks), wcol(tz, z0),
               wcol(tc, gc0), wcol(tc, gc0 + conv_blocks)]
    col = lambda r, w: pl.BlockSpec((r, w), lambda j: (0, j))
    in_specs = ([_resident((rows, d), lambda j: (0, 0))] + w_specs
                + [col(CONV_W, tc),
                   pl.BlockSpec((n_seq, F32_SUBLANES, tc), lambda j: (0, 0, j)),
                   pl.BlockSpec((n_seq, MAX_WIN, tz), lambda j: (0, 0, j))])
    group_widths = (d, d, d, d_pool, d, d)
    out_shape = ([jax.ShapeDtypeStruct((n_j, ns, tc), BF16),
                  jax.ShapeDtypeStruct((n_j, ns, tz), BF16),
                  jax.ShapeDtypeStruct((n_j, ns, tc), F32),
                  jax.ShapeDtypeStruct((n_j, ns, tc), F32),
                  jax.ShapeDtypeStruct((rows, d), F32),
                  jax.ShapeDtypeStruct((rows, d_pool), F32)]
                 + [jax.ShapeDtypeStruct((d, w), BF16) for w in group_widths])
    chunk = lambda w: pl.BlockSpec((1, ns, w), lambda j: (j, 0, 0))
    out_specs = ([chunk(tc), chunk(tz), chunk(tc), chunk(tc), col(rows, tc), col(rows, tz)]
                 + [col(d, tz if w == d_pool else tc) for w in group_widths])
    return pl.pallas_call(
        functools.partial(_mix_sample_body, n_seq=n_seq, n_j=n_j),
        grid=(n_j,),
        in_specs=in_specs,
        out_specs=out_specs,
        out_shape=out_shape,
        compiler_params=_params(("arbitrary",)),
        name="mix_sample",
    )(u, w_in, w_in, w_in, w_in, w_in, w_in, conv_w, cst, zst)


def _join_chunks(ref):
    return jnp.concatenate([ref[c] for c in range(ref.shape[0])], axis=1)


def _merged_block(bc_ref, pooled_ref, sgc_ref, sgp_ref, ps_ref, wco, wpg):
    y_conv = _dot(_join_chunks(bc_ref), wco)
    y_pool = _dot(_join_chunks(pooled_ref), wpg) * ps_ref[...]
    return (sgc_ref[0] * y_conv + sgp_ref[0] * y_pool).astype(BF16)


def _outproj_body(bc_ref, pooled_ref, sgc_ref, sgp_ref, h_ref, wco_ref, wpg_ref, ps_ref, wo_ref,
                  h2_ref, *copies, emit_w, resident_w):
    wco_c, wpg_c, wo_c = copies if emit_w else (None, None, None)
    n = pl.program_id(1)

    @pl.when(n == 0)
    def _():
        h2_ref[...] = h_ref[...]

    if resident_w:
        wco, wpg, wo = wco_ref[n], wpg_ref[n], wo_ref[n]
    else:
        wco, wpg, wo = _weight(wco_ref, wco_c), _weight(wpg_ref, wpg_c), _weight(wo_ref, wo_c)
    m = _merged_block(bc_ref, pooled_ref, sgc_ref, sgp_ref, ps_ref, wco, wpg)
    h2_ref[...] += _dot(m, wo)


def _outproj(bc, pooled, sgc, sgp, h, wco, wpg, pscale, wo, *, tm, tn, name):
    bc_chunks, rows, bc_width = bc.shape
    d = bc_chunks * bc_width
    n_groups, pool_in, group_out = wpg.shape
    n_i, n_n = rows // tm, d // tn
    per_group = group_out // tn
    pool_chunks = pooled.shape[0] // n_groups
    emit_w = wco.dtype == F32
    resident_w = not emit_w
    assert rows == n_i * tm and d == n_n * tn and group_out == per_group * tn
    assert sgc.shape == sgp.shape == (n_n, rows, tn) and pool_chunks * pooled.shape[2] == pool_in
    row_spec = _resident if n_i == 1 else pl.BlockSpec
    if resident_w:
        assert per_group == 1 and wco.shape == (n_n, d, tn) and wo.shape == (n_n, tn, d)
        whole = lambda a: _resident(a.shape, lambda i, n: (0, 0, 0))
        w_specs = [whole(wco), whole(wpg), whole(wo)]
        w_shapes, w_out_specs = [], []
    else:
        assert n_i == 1 and d == n_groups * group_out
        chunk_spec = lambda r: pl.BlockSpec((1, r, tn), lambda i, n: (n // per_group, 0, n % per_group))
        w_specs = [pl.BlockSpec((d, tn), lambda i, n: (0, n)), chunk_spec(pool_in),
                   pl.BlockSpec((tn, d), lambda i, n: (n, 0))]
        w_shapes = [jax.ShapeDtypeStruct((n_groups, d, group_out), BF16),
                    jax.ShapeDtypeStruct(wpg.shape, BF16), jax.ShapeDtypeStruct(wo.shape, BF16)]
        w_out_specs = [chunk_spec(d), w_specs[1], w_specs[2]]
    in_specs = [
        row_spec((bc_chunks, tm, bc_width), lambda i, n: (0, i, 0)),
        pl.BlockSpec((pool_chunks, tm, pooled.shape[2]), lambda i, n: (n // per_group, i, 0)),
        pl.BlockSpec((1, tm, tn), lambda i, n: (n, i, 0)),
        pl.BlockSpec((1, tm, tn), lambda i, n: (n, i, 0)),
        row_spec((tm, d), lambda i, n: (i, 0)),
        w_specs[0], w_specs[1],
        pl.BlockSpec((1, tn), lambda i, n: (0, n)),
        w_specs[2],
    ]
    return pl.pallas_call(
        functools.partial(_outproj_body, emit_w=emit_w, resident_w=resident_w),
        grid=(n_i, n_n),
        in_specs=in_specs,
        out_specs=[row_spec((tm, d), lambda i, n: (i, 0))] + w_out_specs,
        out_shape=[jax.ShapeDtypeStruct((rows, d), F32)] + w_shapes,
        compiler_params=_params(("arbitrary", "arbitrary")),
        name=name,
    )(bc, pooled, sgc, sgp, h, wco, wpg, pscale, wo)


def kernel(x_prompt, x_sample, state_conv, state_pool, meta_tokens, norm_ffn1, w_ffn1_gate,
           w_ffn1_up, w_ffn1_down, norm_mix, w_in, conv_w, w_conv_out, w_pool_group, pool_scale,
           w_o, norm_ffn2, w_ffn2_gate, w_ffn2_up, w_ffn2_down, norm_final):
    batch, seq, d = x_prompt.shape
    dec_batch, dec_seq, _ = x_sample.shape
    depth = w_in.shape[0]
    d_pool = state_pool.shape[-1]
    assert depth == 1 and dec_seq == F32_SUBLANES and meta_tokens.shape[0] == N_META == MAX_WIN
    assert state_conv.shape[2] == CONV_W - 1 and state_pool.shape[2] == MAX_WIN - 1
    n_s = dec_batch * dec_seq

    row = lambda v: v.reshape(1, d)
    g1, gm, g2, gf = row(norm_ffn1[0]), row(norm_mix[0]), row(norm_ffn2[0]), row(norm_final)
    pscale = row(pool_scale[0])
    cw = conv_w[0]

    xs = jnp.concatenate([x_sample.reshape(n_s, d), meta_tokens.astype(F32)], axis=0)
    h_s, u_s, wg1, wu1, wd1 = _ffn(xs, g1, w_ffn1_gate[0], w_ffn1_up[0], w_ffn1_down[0], gm,
                                   tm=xs.shape[0], tf=256, emit_h=True, name="ffn1_sample", w_chunk=512)
    cst = jnp.pad(state_conv[0], ((0, 0), (F32_SUBLANES - (CONV_W - 1), 0), (0, 0)))
    zst = jnp.pad(state_pool[0], ((0, 0), (1, 0), (0, 0)))
    bc_s, pooled_s, sgc_s, sgp_s, cv_all, z_all, *w_groups = _mix_sample(u_s, w_in[0], cw, cst, zst, n_j=8)
    h2_s, wco, wpg, wo = _outproj(bc_s, pooled_s, sgc_s, sgp_s, h_s, w_conv_out[0], w_pool_group[0],
                                  pscale, w_o[0], tm=n_s, tn=256, name="outproj_sample")
    y_s, wg2, wu2, wd2 = _ffn(h2_s, g2, w_ffn2_gate[0], w_ffn2_up[0], w_ffn2_down[0], gf,
                              tm=n_s, tf=256, emit_h=False, name="ffn2_sample", w_chunk=512)

    cv_s = cv_all[:n_s].reshape(dec_batch, dec_seq, d)
    new_conv_sample = cv_s[:, dec_seq - (CONV_W - 1):]
    new_pool_sample = jnp.concatenate(
        [state_pool[0][:, dec_seq:], z_all[:n_s].reshape(dec_batch, dec_seq, d_pool)], axis=1)

    cinit = jnp.pad(cv_all[n_s + N_META - (CONV_W - 1):], ((F32_SUBLANES - (CONV_W - 1), 0), (0, 0)))
    zinit = z_all[n_s:]
    xp = x_prompt.reshape(batch * seq, d)
    h_p, u_p = _ffn(xp, g1, wg1, wu1, wd1, gm, tm=512, tf=wg1.shape[2], emit_h=True, name="ffn1_prompt")
    bc_p, pooled_p, sgc_p, sgp_p, cvlast, zlast = _mix_prompt(u_p, w_groups, cw, cinit, zinit,
                                                              n_seq=batch, tm=1024, n_j=4)
    tn = wpg.shape[2]
    (h2_p,) = _outproj(bc_p, pooled_p, sgc_p, sgp_p, h_p, wco, wpg, pscale, wo.reshape(d // tn, tn, d),
                       tm=512, tn=tn, name="outproj_prompt")
    (y_p,) = _ffn(h2_p, g2, wg2, wu2, wd2, gf, tm=512, tf=wg2.shape[2], emit_h=False, name="ffn2_prompt")

    return (y_p.reshape(batch, seq, d),
            y_s.reshape(dec_batch, dec_seq, d),
            cvlast[None],
            zlast[:, 1:][None],
            new_conv_sample[None],
            new_pool_sample[None])
```

```python
import functools

import jax
import jax.numpy as jnp
from jax import lax
from jax.experimental import pallas as pl
from jax.experimental.pallas import tpu as pltpu

F32 = jnp.float32
BF16 = jnp.bfloat16

EPS = 1e-6
N_META = 16
CONV_W = 3
POOL_WINDOWS = (2, 4, 8, 16)
MAX_WIN = max(POOL_WINDOWS)
N_GROUPS = len(POOL_WINDOWS)

F32_SUBLANES = 8
V7X_VMEM_LIMIT_BYTES = 60 * 1024 * 1024


def _params(semantics):
    return pltpu.CompilerParams(dimension_semantics=semantics,
                                vmem_limit_bytes=V7X_VMEM_LIMIT_BYTES)


def _dot(a, b):
    return jnp.dot(a, b, preferred_element_type=F32)


def _rms(x, g):
    ms = jnp.mean(x * x, axis=-1, keepdims=True)
    return (x * lax.rsqrt(ms + EPS)) * g


def _resident(shape, index_map):
    return pl.BlockSpec(shape, index_map, pipeline_mode=pl.Buffered(1))


def _block(ref):
    return ref[0] if len(ref.shape) == 3 else ref[...]


def _set_block(ref, value):
    if len(ref.shape) == 3:
        ref[0] = value
    else:
        ref[...] = value


def _weight(w_ref, copy_ref):
    w = _block(w_ref).astype(BF16)
    if copy_ref is not None:
        _set_block(copy_ref, w)
    return w


def _ffn_body(x_ref, g_ref, wg_ref, wu_ref, wd_ref, g2_ref, *refs, emit_h, emit_w):
    refs = list(refs)
    acc_ref = refs.pop(0)
    n_ref = refs.pop(0) if emit_h else acc_ref
    copies = [refs.pop(0) for _ in range(3)] if emit_w else [None] * 3
    (xn_ref,) = refs
    f = pl.program_id(1)

    @pl.when(f == 0)
    def _():
        xn_ref[...] = _rms(x_ref[...], g_ref[...]).astype(BF16)
        acc_ref[...] = jnp.zeros(acc_ref.shape, F32)

    xn = xn_ref[...]
    a = _dot(xn, _weight(wg_ref, copies[0]))
    b = _dot(xn, _weight(wu_ref, copies[1]))
    act = ((a * jax.nn.sigmoid(a)) * b).astype(BF16)
    acc_ref[...] += _dot(act, _weight(wd_ref, copies[2]))

    @pl.when(f == pl.num_programs(1) - 1)
    def _():
        h = x_ref[...] + 0.5 * acc_ref[...]
        if emit_h:
            acc_ref[...] = h
        n_ref[...] = _rms(h, g2_ref[...]).astype(n_ref.dtype)


def _ffn(x, g, wg, wu, wd, g2, *, tm, tf, emit_h, name, w_chunk=None):
    rows, d = x.shape
    d_ff = wd.shape[0]
    n_i, n_f = rows // tm, d_ff // tf
    emit_w = wg.dtype == F32
    assert rows == n_i * tm and d_ff == n_f * tf
    row_spec = _resident if n_i == 1 else pl.BlockSpec
    row = lambda i, f: (i, 0)
    wd_spec = pl.BlockSpec((tf, d), lambda i, f: (f, 0))
    if emit_w:
        assert n_i == 1 and w_chunk % tf == 0
        per_chunk = w_chunk // tf
        col_spec = pl.BlockSpec((d, tf), lambda i, f: (0, f))
        w_specs = [col_spec, col_spec, wd_spec]
        chunk_spec = pl.BlockSpec((1, d, tf), lambda i, f: (f // per_chunk, 0, f % per_chunk))
        chunked = jax.ShapeDtypeStruct((d_ff // w_chunk, d, w_chunk), BF16)
        w_shapes = [chunked, chunked, jax.ShapeDtypeStruct(wd.shape, BF16)]
        w_out_specs = [chunk_spec, chunk_spec, wd_spec]
    else:
        assert wg.shape == wu.shape == (n_f, d, tf)
        chunk_spec = pl.BlockSpec((1, d, tf), lambda i, f: (f, 0, 0))
        w_specs = [chunk_spec, chunk_spec, wd_spec]
        w_shapes, w_out_specs = [], []
    in_specs = ([row_spec((tm, d), row), pl.BlockSpec((1, d), lambda i, f: (0, 0))] + w_specs
                + [pl.BlockSpec((1, d), lambda i, f: (0, 0))])
    out_shape = [jax.ShapeDtypeStruct((rows, d), F32)]
    out_specs = [row_spec((tm, d), row)]
    if emit_h:
        out_shape.append(jax.ShapeDtypeStruct((rows, d), BF16))
        out_specs.append(row_spec((tm, d), row))
    return pl.pallas_call(
        functools.partial(_ffn_body, emit_h=emit_h, emit_w=emit_w),
        grid=(n_i, n_f),
        in_specs=in_specs,
        out_specs=out_specs + w_out_specs,
        out_shape=out_shape + w_shapes,
        scratch_shapes=[pltpu.VMEM((tm, d), BF16)],
        compiler_params=_params(("arbitrary", "arbitrary")),
        name=name,
    )(x, g, wg, wu, wd, g2)


W_SLOTS = 2


def _ffn_stream_body(x_ref, g_ref, wg_hbm, wu_hbm, wd_hbm, g2_ref, *refs, emit_h, n_f):
    refs = list(refs)
    acc_ref = refs.pop(0)
    n_ref = refs.pop(0) if emit_h else acc_ref
    xn_ref, wg_buf, wu_buf, wd_buf, sem = refs
    i = pl.program_id(0)

    def gate_up(c):
        chunk, slot = c % n_f, (i * n_f + c) % W_SLOTS
        return (pltpu.make_async_copy(wg_hbm.at[chunk], wg_buf.at[slot], sem.at[0, slot]),
                pltpu.make_async_copy(wu_hbm.at[chunk], wu_buf.at[slot], sem.at[1, slot]))

    def down(c):
        chunk, slot = c % n_f, (i * n_f + c) % W_SLOTS
        return (pltpu.make_async_copy(wd_hbm.at[chunk], wd_buf.at[slot], sem.at[2, slot]),)

    @pl.when(i == 0)
    def _():
        for cp in gate_up(0) + gate_up(1) + down(0):
            cp.start()
        for cp in gate_up(0):
            cp.wait()

    xn_ref[...] = _rms(x_ref[...], g_ref[...]).astype(BF16)
    for f in range(n_f):
        slot = (i * n_f + f) % W_SLOTS
        xn = xn_ref[...]
        a = _dot(xn, wg_buf[slot])
        b = _dot(xn, wu_buf[slot])
        act = ((a * jax.nn.sigmoid(a)) * b).astype(BF16)
        for cp in down(f) + gate_up(f + 1):
            cp.wait()
        for cp in gate_up(f + 2) + down(f + 1):
            cp.start()
        part = _dot(act, wd_buf[slot])
        if f == 0:
            acc_ref[...] = part
        else:
            acc_ref[...] += part

    h = x_ref[...] + 0.5 * acc_ref[...]
    if emit_h:
        acc_ref[...] = h
    n_ref[...] = _rms(h, g2_ref[...]).astype(n_ref.dtype)

    @pl.when(i == pl.num_programs(0) - 1)
    def _():
        for cp in gate_up(n_f + 1) + down(n_f):
            cp.wait()


def _ffn_stream(x, g, wg, wu, wd, g2, *, tm, emit_h, name):
    rows, d = x.shape
    n_f, _, tf = wg.shape
    n_i = rows // tm
    assert rows == n_i * tm and wu.shape == wg.shape and wd.shape == (n_f, tf, d)
    assert n_f >= W_SLOTS
    row = lambda i: (i, 0)
    hbm = pl.BlockSpec(memory_space=pl.ANY)
    in_specs = [pl.BlockSpec((tm, d), row), pl.BlockSpec((1, d), lambda i: (0, 0)), hbm, hbm, hbm,
                pl.BlockSpec((1, d), lambda i: (0, 0))]
    out_shape = [jax.ShapeDtypeStruct((rows, d), F32)]
    out_specs = [pl.BlockSpec((tm, d), row)]
    if emit_h:
        out_shape.append(jax.ShapeDtypeStruct((rows, d), BF16))
        out_specs.append(pl.BlockSpec((tm, d), row))
    scratch = [pltpu.VMEM((tm, d), BF16),
               pltpu.VMEM((W_SLOTS, d, tf), BF16), pltpu.VMEM((W_SLOTS, d, tf), BF16),
               pltpu.VMEM((W_SLOTS, tf, d), BF16),
               pltpu.SemaphoreType.DMA((3, W_SLOTS))]
    return pl.pallas_call(
        functools.partial(_ffn_stream_body, emit_h=emit_h, n_f=n_f),
        grid=(n_i,),
        in_specs=in_specs,
        out_specs=out_specs,
        out_shape=out_shape,
        scratch_shapes=scratch,
        compiler_params=_params(("arbitrary",)),
        name=name,
    )(x, g, wg, wu, wd, g2)


def _conv3(w_ref, ext):
    w = w_ref[...]
    y = w[0:1] * pltpu.roll(ext, 2, axis=0)
    y = y + w[1:2] * pltpu.roll(ext, 1, axis=0)
    return y + w[2:3] * ext


def _window_sums(e, g):
    p = e + pltpu.roll(e, 1, axis=0)
    for level, shift in ((1, 2), (2, 4), (3, 8)):
        p = jnp.where(g >= level, p + pltpu.roll(p, shift, axis=0), p)
    return p


def _inv_window(g):
    return jnp.where(g == 0, 0.5, jnp.where(g == 1, 0.25, jnp.where(g == 2, 0.125, 0.0625))).astype(F32)


def _mix_prompt_body(u_ref, wb_ref, wc_ref, wv_ref, wz_ref, wgc_ref, wgp_ref, cw_ref, cinit_ref,
                     zinit_ref, bc_ref, pooled_ref, sgc_ref, sgp_ref, cvlast_ref, zlast_ref,
                     cpast_ref, zpast_ref, *, tm, tiles_per_seq, n_j):
    j = pl.program_id(0)
    i = pl.program_id(1)
    t = i % tiles_per_seq
    b = i // tiles_per_seq
    g = j // (n_j // N_GROUPS)
    u = u_ref[...]

    @pl.when(t == 0)
    def _():
        cpast_ref[...] = cinit_ref[...]
        zpast_ref[...] = zinit_ref[...]

    cv = _dot(u, wc_ref[...]) * _dot(u, wv_ref[...])
    ext = jnp.concatenate([cpast_ref[...], cv], axis=0)
    y = _conv3(cw_ref, ext)[F32_SUBLANES:]
    bc_ref[0] = (_dot(u, wb_ref[...]) * y).astype(BF16)
    cpast_ref[...] = cv[tm - F32_SUBLANES:]

    z = _dot(u, wz_ref[...])
    sums = _window_sums(jnp.concatenate([zpast_ref[...], z], axis=0), g)[MAX_WIN:]
    pooled_ref[0] = (sums * _inv_window(g) - z).astype(BF16)
    zpast_ref[...] = z[tm - MAX_WIN:]

    @pl.when(t == tiles_per_seq - 1)
    def _():
        cvlast_ref[b] = cpast_ref[pl.ds(F32_SUBLANES - (CONV_W - 1), CONV_W - 1), :]
        zlast_ref[b] = zpast_ref[...]

    sgc_ref[0] = jax.nn.sigmoid(_dot(u, wgc_ref[...]))
    sgp_ref[0] = jax.nn.sigmoid(_dot(u, wgp_ref[...]))


def _mix_prompt(u, w_groups, conv_w, cinit, zinit, *, n_seq, tm, n_j):
    rows, d = u.shape
    d_pool = zinit.shape[1]
    tc, tz = d // n_j, d_pool // n_j
    seq = rows // n_seq
    tiles_per_seq = seq // tm
    n_i = rows // tm
    assert seq == tiles_per_seq * tm and n_j % N_GROUPS == 0 and tm >= MAX_WIN
    blk = lambda w: pl.BlockSpec((1, tm, w), lambda j, i: (j, i, 0))
    wcol = lambda w: pl.BlockSpec((d, w), lambda j, i: (0, j))
    in_specs = ([pl.BlockSpec((tm, d), lambda j, i: (i, 0))]
                + [wcol(tc), wcol(tc), wcol(tc), wcol(tz), wcol(tc), wcol(tc)]
                + [pl.BlockSpec((CONV_W, tc), lambda j, i: (0, j)),
                   pl.BlockSpec((F32_SUBLANES, tc), lambda j, i: (0, j)),
                   pl.BlockSpec((MAX_WIN, tz), lambda j, i: (0, j))])
    out_shape = (jax.ShapeDtypeStruct((n_j, rows, tc), BF16),
                 jax.ShapeDtypeStruct((n_j, rows, tz), BF16),
                 jax.ShapeDtypeStruct((n_j, rows, tc), F32),
                 jax.ShapeDtypeStruct((n_j, rows, tc), F32),
                 jax.ShapeDtypeStruct((n_seq, CONV_W - 1, d), F32),
                 jax.ShapeDtypeStruct((n_seq, MAX_WIN, d_pool), F32))
    out_specs = (blk(tc), blk(tz), blk(tc), blk(tc),
                 pl.BlockSpec((n_seq, CONV_W - 1, tc), lambda j, i: (0, 0, j)),
                 pl.BlockSpec((n_seq, MAX_WIN, tz), lambda j, i: (0, 0, j)))
    scratch = [pltpu.VMEM((F32_SUBLANES, tc), F32), pltpu.VMEM((MAX_WIN, tz), F32)]
    return pl.pallas_call(
        functools.partial(_mix_prompt_body, tm=tm, tiles_per_seq=tiles_per_seq, n_j=n_j),
        grid=(n_j, n_i),
        in_specs=in_specs,
        out_specs=out_specs,
        out_shape=out_shape,
        scratch_shapes=scratch,
        compiler_params=_params(("arbitrary", "arbitrary")),
        name="mix_prompt",
    )(u, *w_groups, conv_w, cinit, zinit)


def _mix_sample_body(u_ref, wb_ref, wc_ref, wv_ref, wz_ref, wgc_ref, wgp_ref, cw_ref, cst_ref,
                     zst_ref, bc_ref, pooled_ref, sgc_ref, sgp_ref, cvall_ref, zall_ref,
                     wb_c, wc_c, wv_c, wz_c, wgc_c, wgp_c, *, n_seq, n_j):
    seq = F32_SUBLANES
    ns = n_seq * seq
    j = pl.program_id(0)
    g = j // (n_j // N_GROUPS)
    u = u_ref[...]
    us = u[:ns]
    tc = cvall_ref.shape[1]
    tz = zall_ref.shape[1]

    cv_all = _dot(u, _weight(wc_ref, wc_c)) * _dot(u, _weight(wv_ref, wv_c))
    cvall_ref[...] = cv_all
    ext = jnp.concatenate([cst_ref[...], cv_all[:ns].reshape(n_seq, seq, tc)], axis=1)
    y = _conv3(cw_ref, ext.reshape(2 * ns, tc))
    y = y.reshape(n_seq, 2 * seq, tc)[:, seq:, :].reshape(ns, tc)
    bc_ref[0] = (_dot(us, _weight(wb_ref, wb_c)) * y).astype(BF16)

    z_all = _dot(u, _weight(wz_ref, wz_c))
    zall_ref[...] = z_all
    z = z_all[:ns]
    per = MAX_WIN + seq
    ext = jnp.concatenate([zst_ref[...], z.reshape(n_seq, seq, tz)], axis=1)
    sums = _window_sums(ext.reshape(n_seq * per, tz), g)
    sums = sums.reshape(n_seq, per, tz)[:, MAX_WIN:, :].reshape(ns, tz)
    pooled_ref[0] = (sums * _inv_window(g) - z).astype(BF16)

    sgc_ref[0] = jax.nn.sigmoid(_dot(us, _weight(wgc_ref, wgc_c)))
    sgp_ref[0] = jax.nn.sigmoid(_dot(us, _weight(wgp_ref, wgp_c)))


def _mix_sample(u, w_in, conv_w, cst, zst, *, n_j):
    rows, d = u.shape
    n_seq, _, d_pool = zst.shape
    ns = n_seq * F32_SUBLANES
    tc, tz = d // n_j, d_pool // n_j
    assert n_j % N_GROUPS == 0 and cst.shape == (n_seq, F32_SUBLANES, d)
    assert (3 * d + d_pool) % tc == 0
    conv_blocks = d // tc
    z0 = (3 * d) // tz
    gc0 = (3 * d + d_pool) // tc
    wcol = lambda w, off: pl.BlockSpec((d, w), lambda j: (0, off + j))
    w_specs = [wcol(tc, 0), wcol(tc, conv_blocks), wcol(tc, 2 * conv_blocks), wcol(tz, z0),
               wcol(tc, gc0), wcol(tc, gc0 + conv_blocks)]
    col = lambda r, w: pl.BlockSpec((r, w), lambda j: (0, j))
    in_specs = ([_resident((rows, d), lambda j: (0, 0))] + w_specs
                + [col(CONV_W, tc),
                   pl.BlockSpec((n_seq, F32_SUBLANES, tc), lambda j: (0, 0, j)),
                   pl.BlockSpec((n_seq, MAX_WIN, tz), lambda j: (0, 0, j))])
    group_widths = (d, d, d, d_pool, d, d)
    out_shape = ([jax.ShapeDtypeStruct((n_j, ns, tc), BF16),
                  jax.ShapeDtypeStruct((n_j, ns, tz), BF16),
                  jax.ShapeDtypeStruct((n_j, ns, tc), F32),
                  jax.ShapeDtypeStruct((n_j, ns, tc), F32),
                  jax.ShapeDtypeStruct((rows, d), F32),
                  jax.ShapeDtypeStruct((rows, d_pool), F32)]
                 + [jax.ShapeDtypeStruct((d, w), BF16) for w in group_widths])
    chunk = lambda w: pl.BlockSpec((1, ns, w), lambda j: (j, 0, 0))
    out_specs = ([chunk(tc), chunk(tz), chunk(tc), chunk(tc), col(rows, tc), col(rows, tz)]
                 + [col(d, tz if w == d_pool else tc) for w in group_widths])
    return pl.pallas_call(
        functools.partial(_mix_sample_body, n_seq=n_seq, n_j=n_j),
        grid=(n_j,),
        in_specs=in_specs,
        out_specs=out_specs,
        out_shape=out_shape,
        compiler_params=_params(("arbitrary",)),
        name="mix_sample",
    )(u, w_in, w_in, w_in, w_in, w_in, w_in, conv_w, cst, zst)


def _join_chunks(ref):
    return jnp.concatenate([ref[c] for c in range(ref.shape[0])], axis=1)


def _merged_block(bc_ref, pooled_ref, sgc_ref, sgp_ref, ps_ref, wco, wpg):
    y_conv = _dot(_join_chunks(bc_ref), wco)
    y_pool = _dot(_join_chunks(pooled_ref), wpg) * ps_ref[...]
    return (sgc_ref[0] * y_conv + sgp_ref[0] * y_pool).astype(BF16)


def _outproj_body(bc_ref, pooled_ref, sgc_ref, sgp_ref, h_ref, wco_ref, wpg_ref, ps_ref, wo_ref,
                  h2_ref, *copies, emit_w, resident_w):
    wco_c, wpg_c, wo_c = copies if emit_w else (None, None, None)
    n = pl.program_id(1)

    @pl.when(n == 0)
    def _():
        h2_ref[...] = h_ref[...]

    if resident_w:
        wco, wpg, wo = wco_ref[n], wpg_ref[n], wo_ref[n]
    else:
        wco, wpg, wo = _weight(wco_ref, wco_c), _weight(wpg_ref, wpg_c), _weight(wo_ref, wo_c)
    m = _merged_block(bc_ref, pooled_ref, sgc_ref, sgp_ref, ps_ref, wco, wpg)
    h2_ref[...] += _dot(m, wo)


def _outproj(bc, pooled, sgc, sgp, h, wco, wpg, pscale, wo, *, tm, tn, name):
    bc_chunks, rows, bc_width = bc.shape
    d = bc_chunks * bc_width
    n_groups, pool_in, group_out = wpg.shape
    n_i, n_n = rows // tm, d // tn
    per_group = group_out // tn
    pool_chunks = pooled.shape[0] // n_groups
    emit_w = wco.dtype == F32
    resident_w = not emit_w
    assert rows == n_i * tm and d == n_n * tn and group_out == per_group * tn
    assert sgc.shape == sgp.shape == (n_n, rows, tn) and pool_chunks * pooled.shape[2] == pool_in
    row_spec = _resident if n_i == 1 else pl.BlockSpec
    if resident_w:
        assert per_group == 1 and wco.shape == (n_n, d, tn) and wo.shape == (n_n, tn, d)
        whole = lambda a: _resident(a.shape, lambda i, n: (0, 0, 0))
        w_specs = [whole(wco), whole(wpg), whole(wo)]
        w_shapes, w_out_specs = [], []
    else:
        assert n_i == 1 and d == n_groups * group_out
        chunk_spec = lambda r: pl.BlockSpec((1, r, tn), lambda i, n: (n // per_group, 0, n % per_group))
        w_specs = [pl.BlockSpec((d, tn), lambda i, n: (0, n)), chunk_spec(pool_in),
                   pl.BlockSpec((tn, d), lambda i, n: (n, 0))]
        w_shapes = [jax.ShapeDtypeStruct((n_groups, d, group_out), BF16),
                    jax.ShapeDtypeStruct(wpg.shape, BF16), jax.ShapeDtypeStruct(wo.shape, BF16)]
        w_out_specs = [chunk_spec(d), w_specs[1], w_specs[2]]
    in_specs = [
        row_spec((bc_chunks, tm, bc_width), lambda i, n: (0, i, 0)),
        pl.BlockSpec((pool_chunks, tm, pooled.shape[2]), lambda i, n: (n // per_group, i, 0)),
        pl.BlockSpec((1, tm, tn), lambda i, n: (n, i, 0)),
        pl.BlockSpec((1, tm, tn), lambda i, n: (n, i, 0)),
        row_spec((tm, d), lambda i, n: (i, 0)),
        w_specs[0], w_specs[1],
        pl.BlockSpec((1, tn), lambda i, n: (0, n)),
        w_specs[2],
    ]
    return pl.pallas_call(
        functools.partial(_outproj_body, emit_w=emit_w, resident_w=resident_w),
        grid=(n_i, n_n),
        in_specs=in_specs,
        out_specs=[row_spec((tm, d), lambda i, n: (i, 0))] + w_out_specs,
        out_shape=[jax.ShapeDtypeStruct((rows, d), F32)] + w_shapes,
        compiler_params=_params(("arbitrary", "arbitrary")),
        name=name,
    )(bc, pooled, sgc, sgp, h, wco, wpg, pscale, wo)


def kernel(x_prompt, x_sample, state_conv, state_pool, meta_tokens, norm_ffn1, w_ffn1_gate,
           w_ffn1_up, w_ffn1_down, norm_mix, w_in, conv_w, w_conv_out, w_pool_group, pool_scale,
           w_o, norm_ffn2, w_ffn2_gate, w_ffn2_up, w_ffn2_down, norm_final):
    batch, seq, d = x_prompt.shape
    dec_batch, dec_seq, _ = x_sample.shape
    depth = w_in.shape[0]
    d_pool = state_pool.shape[-1]
    assert depth == 1 and dec_seq == F32_SUBLANES and meta_tokens.shape[0] == N_META == MAX_WIN
    assert state_conv.shape[2] == CONV_W - 1 and state_pool.shape[2] == MAX_WIN - 1
    n_s = dec_batch * dec_seq

    row = lambda v: v.reshape(1, d)
    g1, gm, g2, gf = row(norm_ffn1[0]), row(norm_mix[0]), row(norm_ffn2[0]), row(norm_final)
    pscale = row(pool_scale[0])
    cw = conv_w[0]

    xs = jnp.concatenate([x_sample.reshape(n_s, d), meta_tokens.astype(F32)], axis=0)
    h_s, u_s, wg1, wu1, wd1 = _ffn(xs, g1, w_ffn1_gate[0], w_ffn1_up[0], w_ffn1_down[0], gm,
                                   tm=xs.shape[0], tf=256, emit_h=True, name="ffn1_sample", w_chunk=512)
    cst = jnp.pad(state_conv[0], ((0, 0), (F32_SUBLANES - (CONV_W - 1), 0), (0, 0)))
    zst = jnp.pad(state_pool[0], ((0, 0), (1, 0), (0, 0)))
    bc_s, pooled_s, sgc_s, sgp_s, cv_all, z_all, *w_groups = _mix_sample(u_s, w_in[0], cw, cst, zst, n_j=8)
    h2_s, wco, wpg, wo = _outproj(bc_s, pooled_s, sgc_s, sgp_s, h_s, w_conv_out[0], w_pool_group[0],
                                  pscale, w_o[0], tm=n_s, tn=256, name="outproj_sample")
    y_s, wg2, wu2, wd2 = _ffn(h2_s, g2, w_ffn2_gate[0], w_ffn2_up[0], w_ffn2_down[0], gf,
                              tm=n_s, tf=256, emit_h=False, name="ffn2_sample", w_chunk=512)

    cv_s = cv_all[:n_s].reshape(dec_batch, dec_seq, d)
    new_conv_sample = cv_s[:, dec_seq - (CONV_W - 1):]
    new_pool_sample = jnp.concatenate(
        [state_pool[0][:, dec_seq:], z_all[:n_s].reshape(dec_batch, dec_seq, d_pool)], axis=1)

    cinit = jnp.pad(cv_all[n_s + N_META - (CONV_W - 1):], ((F32_SUBLANES - (CONV_W - 1), 0), (0, 0)))
    zinit = z_all[n_s:]
    xp = x_prompt.reshape(batch * seq, d)
    down_chunks = lambda wd, wg: wd.reshape(wg.shape[0], wg.shape[2], d)
    h_p, u_p = _ffn_stream(xp, g1, wg1, wu1, down_chunks(wd1, wg1), gm, tm=512, emit_h=True,
                           name="ffn1_prompt")
    bc_p, pooled_p, sgc_p, sgp_p, cvlast, zlast = _mix_prompt(u_p, w_groups, cw, cinit, zinit,
                                                              n_seq=batch, tm=1024, n_j=4)
    tn = wpg.shape[2]
    (h2_p,) = _outproj(bc_p, pooled_p, sgc_p, sgp_p, h_p, wco, wpg, pscale, wo.reshape(d // tn, tn, d),
                       tm=512, tn=tn, name="outproj_prompt")
    (y_p,) = _ffn_stream(h2_p, g2, wg2, wu2, down_chunks(wd2, wg2), gf, tm=512, emit_h=False,
                         name="ffn2_prompt")

    return (y_p.reshape(batch, seq, d),
            y_s.reshape(dec_batch, dec_seq, d),
            cvlast[None],
            zlast[:, 1:][None],
            new_conv_sample[None],
            new_pool_sample[None])
```

```python
import functools

import jax
import jax.numpy as jnp
from jax import lax
from jax.experimental import pallas as pl
from jax.experimental.pallas import tpu as pltpu

F32 = jnp.float32
BF16 = jnp.bfloat16

EPS = 1e-6
N_META = 16
CONV_W = 3
POOL_WINDOWS = (2, 4, 8, 16)
MAX_WIN = max(POOL_WINDOWS)
N_GROUPS = len(POOL_WINDOWS)

F32_SUBLANES = 8
V7X_VMEM_LIMIT_BYTES = 60 * 1024 * 1024


def _params(semantics):
    return pltpu.CompilerParams(dimension_semantics=semantics,
                                vmem_limit_bytes=V7X_VMEM_LIMIT_BYTES)


def _dot(a, b):
    return jnp.dot(a, b, preferred_element_type=F32)


def _rms(x, g):
    ms = jnp.mean(x * x, axis=-1, keepdims=True)
    return (x * lax.rsqrt(ms + EPS)) * g


def _resident(shape, index_map):
    return pl.BlockSpec(shape, index_map, pipeline_mode=pl.Buffered(1))


def _block(ref):
    return ref[0] if len(ref.shape) == 3 else ref[...]


def _set_block(ref, value):
    if len(ref.shape) == 3:
        ref[0] = value
    else:
        ref[...] = value


def _weight(w_ref, copy_ref):
    w = _block(w_ref).astype(BF16)
    if copy_ref is not None:
        _set_block(copy_ref, w)
    return w


def _ffn_body(x_ref, g_ref, wg_ref, wu_ref, wd_ref, g2_ref, *refs, emit_h, emit_w):
    refs = list(refs)
    acc_ref = refs.pop(0)
    n_ref = refs.pop(0) if emit_h else acc_ref
    copies = [refs.pop(0) for _ in range(3)] if emit_w else [None] * 3
    (xn_ref,) = refs
    f = pl.program_id(1)

    @pl.when(f == 0)
    def _():
        xn_ref[...] = _rms(x_ref[...], g_ref[...]).astype(BF16)
        acc_ref[...] = jnp.zeros(acc_ref.shape, F32)

    xn = xn_ref[...]
    a = _dot(xn, _weight(wg_ref, copies[0]))
    b = _dot(xn, _weight(wu_ref, copies[1]))
    act = ((a * jax.nn.sigmoid(a)) * b).astype(BF16)
    acc_ref[...] += _dot(act, _weight(wd_ref, copies[2]))

    @pl.when(f == pl.num_programs(1) - 1)
    def _():
        h = x_ref[...] + 0.5 * acc_ref[...]
        if emit_h:
            acc_ref[...] = h
        n_ref[...] = _rms(h, g2_ref[...]).astype(n_ref.dtype)


def _ffn(x, g, wg, wu, wd, g2, *, tm, tf, emit_h, name, w_chunk=None):
    rows, d = x.shape
    d_ff = wd.shape[0]
    n_i, n_f = rows // tm, d_ff // tf
    emit_w = wg.dtype == F32
    assert rows == n_i * tm and d_ff == n_f * tf
    row_spec = _resident if n_i == 1 else pl.BlockSpec
    row = lambda i, f: (i, 0)
    wd_spec = pl.BlockSpec((tf, d), lambda i, f: (f, 0))
    if emit_w:
        assert n_i == 1 and w_chunk % tf == 0
        per_chunk = w_chunk // tf
        col_spec = pl.BlockSpec((d, tf), lambda i, f: (0, f))
        w_specs = [col_spec, col_spec, wd_spec]
        chunk_spec = pl.BlockSpec((1, d, tf), lambda i, f: (f // per_chunk, 0, f % per_chunk))
        chunked = jax.ShapeDtypeStruct((d_ff // w_chunk, d, w_chunk), BF16)
        w_shapes = [chunked, chunked, jax.ShapeDtypeStruct(wd.shape, BF16)]
        w_out_specs = [chunk_spec, chunk_spec, wd_spec]
    else:
        assert wg.shape == wu.shape == (n_f, d, tf)
        chunk_spec = pl.BlockSpec((1, d, tf), lambda i, f: (f, 0, 0))
        w_specs = [chunk_spec, chunk_spec, wd_spec]
        w_shapes, w_out_specs = [], []
    in_specs = ([row_spec((tm, d), row), pl.BlockSpec((1, d), lambda i, f: (0, 0))] + w_specs
                + [pl.BlockSpec((1, d), lambda i, f: (0, 0))])
    out_shape = [jax.ShapeDtypeStruct((rows, d), F32)]
    out_specs = [row_spec((tm, d), row)]
    if emit_h:
        out_shape.append(jax.ShapeDtypeStruct((rows, d), BF16))
        out_specs.append(row_spec((tm, d), row))
    return pl.pallas_call(
        functools.partial(_ffn_body, emit_h=emit_h, emit_w=emit_w),
        grid=(n_i, n_f),
        in_specs=in_specs,
        out_specs=out_specs + w_out_specs,
        out_shape=out_shape + w_shapes,
        scratch_shapes=[pltpu.VMEM((tm, d), BF16)],
        compiler_params=_params(("arbitrary", "arbitrary")),
        name=name,
    )(x, g, wg, wu, wd, g2)


W_SLOTS = 2


def _ffn_stream_body(x_ref, g_ref, wg_hbm, wu_hbm, wd_hbm, g2_ref, *refs, emit_h, units):
    refs = list(refs)
    acc_ref = refs.pop(0)
    n_ref = refs.pop(0) if emit_h else acc_ref
    xn_ref, wg_buf, wu_buf, wd_buf, sem = refs
    i = pl.program_id(0)
    n_units = len(units)

    def slot_of(u):
        return u % W_SLOTS if n_units % W_SLOTS == 0 else (i * n_units + u) % W_SLOTS

    def copy(w_hbm, w_buf, which, u):
        first, count = units[u % n_units]
        slot = slot_of(u)
        return pltpu.make_async_copy(w_hbm.at[pl.ds(first, count)], w_buf.at[slot, pl.ds(0, count)],
                                     sem.at[which, slot])

    def gate_up(u):
        return (copy(wg_hbm, wg_buf, 0, u), copy(wu_hbm, wu_buf, 1, u))

    def down(u):
        return (copy(wd_hbm, wd_buf, 2, u),)

    @pl.when(i == 0)
    def _():
        for cp in gate_up(0) + gate_up(1) + down(0):
            cp.start()
        for cp in gate_up(0):
            cp.wait()

    xn_ref[...] = _rms(x_ref[...], g_ref[...]).astype(BF16)
    for u, (_, count) in enumerate(units):
        slot = slot_of(u)
        xn = xn_ref[...]
        acts = []
        for q in range(count):
            a = _dot(xn, wg_buf[slot, q])
            b = _dot(xn, wu_buf[slot, q])
            acts.append(((a * jax.nn.sigmoid(a)) * b).astype(BF16))
        for cp in down(u) + gate_up(u + 1):
            cp.wait()
        for cp in gate_up(u + 2) + down(u + 1):
            cp.start()
        part = _dot(acts[0], wd_buf[slot, 0])
        for q in range(1, count):
            part = part + _dot(acts[q], wd_buf[slot, q])
        if u == 0:
            acc_ref[...] = part
        else:
            acc_ref[...] += part

    h = x_ref[...] + 0.5 * acc_ref[...]
    if emit_h:
        acc_ref[...] = h
    n_ref[...] = _rms(h, g2_ref[...]).astype(n_ref.dtype)

    @pl.when(i == pl.num_programs(0) - 1)
    def _():
        for cp in gate_up(n_units + 1) + down(n_units):
            cp.wait()


def _ffn_stream(x, g, wg, wu, wd, g2, *, tm, unit, emit_h, name):
    rows, d = x.shape
    n_f, _, tf = wg.shape
    n_i = rows // tm
    assert rows == n_i * tm and wu.shape == wg.shape and wd.shape == (n_f, tf, d)
    units = tuple((first, min(unit, n_f - first)) for first in range(0, n_f, unit))
    assert len(units) >= W_SLOTS
    row = lambda i: (i, 0)
    hbm = pl.BlockSpec(memory_space=pl.ANY)
    in_specs = [pl.BlockSpec((tm, d), row), pl.BlockSpec((1, d), lambda i: (0, 0)), hbm, hbm, hbm,
                pl.BlockSpec((1, d), lambda i: (0, 0))]
    out_shape = [jax.ShapeDtypeStruct((rows, d), F32)]
    out_specs = [pl.BlockSpec((tm, d), row)]
    if emit_h:
        out_shape.append(jax.ShapeDtypeStruct((rows, d), BF16))
        out_specs.append(pl.BlockSpec((tm, d), row))
    scratch = [pltpu.VMEM((tm, d), BF16),
               pltpu.VMEM((W_SLOTS, unit, d, tf), BF16), pltpu.VMEM((W_SLOTS, unit, d, tf), BF16),
               pltpu.VMEM((W_SLOTS, unit, tf, d), BF16),
               pltpu.SemaphoreType.DMA((3, W_SLOTS))]
    return pl.pallas_call(
        functools.partial(_ffn_stream_body, emit_h=emit_h, units=units),
        grid=(n_i,),
        in_specs=in_specs,
        out_specs=out_specs,
        out_shape=out_shape,
        scratch_shapes=scratch,
        compiler_params=_params(("arbitrary",)),
        name=name,
    )(x, g, wg, wu, wd, g2)


def _conv3(w_ref, ext):
    w = w_ref[...]
    y = w[0:1] * pltpu.roll(ext, 2, axis=0)
    y = y + w[1:2] * pltpu.roll(ext, 1, axis=0)
    return y + w[2:3] * ext


def _window_sums(e, g):
    p = e + pltpu.roll(e, 1, axis=0)
    for level, shift in ((1, 2), (2, 4), (3, 8)):
        p = jnp.where(g >= level, p + pltpu.roll(p, shift, axis=0), p)
    return p


def _inv_window(g):
    return jnp.where(g == 0, 0.5, jnp.where(g == 1, 0.25, jnp.where(g == 2, 0.125, 0.0625))).astype(F32)


def _mix_prompt_body(u_ref, wb_ref, wc_ref, wv_ref, wz_ref, wgc_ref, wgp_ref, cw_ref, cinit_ref,
                     zinit_ref, bc_ref, pooled_ref, sgc_ref, sgp_ref, cvlast_ref, zlast_ref,
                     cpast_ref, zpast_ref, *, tm, tiles_per_seq, n_j):
    j = pl.program_id(0)
    i = pl.program_id(1)
    t = i % tiles_per_seq
    b = i // tiles_per_seq
    g = j // (n_j // N_GROUPS)
    u = u_ref[...]

    @pl.when(t == 0)
    def _():
        cpast_ref[...] = cinit_ref[...]
        zpast_ref[...] = zinit_ref[...]

    cv = _dot(u, wc_ref[...]) * _dot(u, wv_ref[...])
    ext = jnp.concatenate([cpast_ref[...], cv], axis=0)
    y = _conv3(cw_ref, ext)[F32_SUBLANES:]
    bc_ref[0] = (_dot(u, wb_ref[...]) * y).astype(BF16)
    cpast_ref[...] = cv[tm - F32_SUBLANES:]

    z = _dot(u, wz_ref[...])
    sums = _window_sums(jnp.concatenate([zpast_ref[...], z], axis=0), g)[MAX_WIN:]
    pooled_ref[0] = (sums * _inv_window(g) - z).astype(BF16)
    zpast_ref[...] = z[tm - MAX_WIN:]

    sgc_ref[0] = jax.nn.sigmoid(_dot(u, wgc_ref[...]))
    sgp_ref[0] = jax.nn.sigmoid(_dot(u, wgp_ref[...]))

    @pl.when(t == tiles_per_seq - 1)
    def _():
        cvlast_ref[b] = cpast_ref[pl.ds(F32_SUBLANES - (CONV_W - 1), CONV_W - 1), :]
        zlast_ref[b] = zpast_ref[...]


def _mix_prompt(u, w_groups, conv_w, cinit, zinit, *, n_seq, tm, n_j):
    rows, d = u.shape
    d_pool = zinit.shape[1]
    tc, tz = d // n_j, d_pool // n_j
    seq = rows // n_seq
    tiles_per_seq = seq // tm
    n_i = rows // tm
    assert seq == tiles_per_seq * tm and n_j % N_GROUPS == 0 and tm >= MAX_WIN
    blk = lambda w: pl.BlockSpec((1, tm, w), lambda j, i: (j, i, 0))
    wcol = lambda w: pl.BlockSpec((d, w), lambda j, i: (0, j))
    in_specs = ([pl.BlockSpec((tm, d), lambda j, i: (i, 0))]
                + [wcol(tc), wcol(tc), wcol(tc), wcol(tz), wcol(tc), wcol(tc)]
                + [pl.BlockSpec((CONV_W, tc), lambda j, i: (0, j)),
                   pl.BlockSpec((F32_SUBLANES, tc), lambda j, i: (0, j)),
                   pl.BlockSpec((MAX_WIN, tz), lambda j, i: (0, j))])
    out_shape = (jax.ShapeDtypeStruct((n_j, rows, tc), BF16),
                 jax.ShapeDtypeStruct((n_j, rows, tz), BF16),
                 jax.ShapeDtypeStruct((n_j, rows, tc), F32),
                 jax.ShapeDtypeStruct((n_j, rows, tc), F32),
                 jax.ShapeDtypeStruct((n_seq, CONV_W - 1, d), F32),
                 jax.ShapeDtypeStruct((n_seq, MAX_WIN, d_pool), F32))
    out_specs = (blk(tc), blk(tz), blk(tc), blk(tc),
                 pl.BlockSpec((n_seq, CONV_W - 1, tc), lambda j, i: (0, 0, j)),
                 pl.BlockSpec((n_seq, MAX_WIN, tz), lambda j, i: (0, 0, j)))
    scratch = [pltpu.VMEM((F32_SUBLANES, tc), F32), pltpu.VMEM((MAX_WIN, tz), F32)]
    return pl.pallas_call(
        functools.partial(_mix_prompt_body, tm=tm, tiles_per_seq=tiles_per_seq, n_j=n_j),
        grid=(n_j, n_i),
        in_specs=in_specs,
        out_specs=out_specs,
        out_shape=out_shape,
        scratch_shapes=scratch,
        compiler_params=_params(("arbitrary", "arbitrary")),
        name="mix_prompt",
    )(u, *w_groups, conv_w, cinit, zinit)


def _mix_sample_body(u_ref, wb_ref, wc_ref, wv_ref, wz_ref, wgc_ref, wgp_ref, cw_ref, cst_ref,
                     zst_ref, bc_ref, pooled_ref, sgc_ref, sgp_ref, cvall_ref, zall_ref,
                     wb_c, wc_c, wv_c, wz_c, wgc_c, wgp_c, *, n_seq, n_j):
    seq = F32_SUBLANES
    ns = n_seq * seq
    j = pl.program_id(0)
    g = j // (n_j // N_GROUPS)
    u = u_ref[...]
    us = u[:ns]
    tc = cvall_ref.shape[1]
    tz = zall_ref.shape[1]

    cv_all = _dot(u, _weight(wc_ref, wc_c)) * _dot(u, _weight(wv_ref, wv_c))
    cvall_ref[...] = cv_all
    ext = jnp.concatenate([cst_ref[...], cv_all[:ns].reshape(n_seq, seq, tc)], axis=1)
    y = _conv3(cw_ref, ext.reshape(2 * ns, tc))
    y = y.reshape(n_seq, 2 * seq, tc)[:, seq:, :].reshape(ns, tc)
    bc_ref[0] = (_dot(us, _weight(wb_ref, wb_c)) * y).astype(BF16)

    z_all = _dot(u, _weight(wz_ref, wz_c))
    zall_ref[...] = z_all
    z = z_all[:ns]
    per = MAX_WIN + seq
    ext = jnp.concatenate([zst_ref[...], z.reshape(n_seq, seq, tz)], axis=1)
    sums = _window_sums(ext.reshape(n_seq * per, tz), g)
    sums = sums.reshape(n_seq, per, tz)[:, MAX_WIN:, :].reshape(ns, tz)
    pooled_ref[0] = (sums * _inv_window(g) - z).astype(BF16)

    sgc_ref[0] = jax.nn.sigmoid(_dot(us, _weight(wgc_ref, wgc_c)))
    sgp_ref[0] = jax.nn.sigmoid(_dot(us, _weight(wgp_ref, wgp_c)))


def _mix_sample(u, w_in, conv_w, cst, zst, *, n_j):
    rows, d = u.shape
    n_seq, _, d_pool = zst.shape
    ns = n_seq * F32_SUBLANES
    tc, tz = d // n_j, d_pool // n_j
    assert n_j % N_GROUPS == 0 and cst.shape == (n_seq, F32_SUBLANES, d)
    assert (3 * d + d_pool) % tc == 0
    conv_blocks = d // tc
    z0 = (3 * d) // tz
    gc0 = (3 * d + d_pool) // tc
    wcol = lambda w, off: pl.BlockSpec((d, w), lambda j: (0, off + j))
    w_specs = [wcol(tc, 0), wcol(tc, conv_blocks), wcol(tc, 2 * conv_blocks), wcol(tz, z0),
               wcol(tc, gc0), wcol(tc, gc0 + conv_blocks)]
    col = lambda r, w: pl.BlockSpec((r, w), lambda j: (0, j))
    in_specs = ([_resident((rows, d), lambda j: (0, 0))] + w_specs
                + [col(CONV_W, tc),
                   pl.BlockSpec((n_seq, F32_SUBLANES, tc), lambda j: (0, 0, j)),
                   pl.BlockSpec((n_seq, MAX_WIN, tz), lambda j: (0, 0, j))])
    group_widths = (d, d, d, d_pool, d, d)
    out_shape = ([jax.ShapeDtypeStruct((n_j, ns, tc), BF16),
                  jax.ShapeDtypeStruct((n_j, ns, tz), BF16),
                  jax.ShapeDtypeStruct((n_j, ns, tc), F32),
                  jax.ShapeDtypeStruct((n_j, ns, tc), F32),
                  jax.ShapeDtypeStruct((rows, d), F32),
                  jax.ShapeDtypeStruct((rows, d_pool), F32)]
                 + [jax.ShapeDtypeStruct((d, w), BF16) for w in group_widths])
    chunk = lambda w: pl.BlockSpec((1, ns, w), lambda j: (j, 0, 0))
    out_specs = ([chunk(tc), chunk(tz), chunk(tc), chunk(tc), col(rows, tc), col(rows, tz)]
                 + [col(d, tz if w == d_pool else tc) for w in group_widths])
    return pl.pallas_call(
        functools.partial(_mix_sample_body, n_seq=n_seq, n_j=n_j),
        grid=(n_j,),
        in_specs=in_specs,
        out_specs=out_specs,
        out_shape=out_shape,
        compiler_params=_params(("arbitrary",)),
        name="mix_sample",
    )(u, w_in, w_in, w_in, w_in, w_in, w_in, conv_w, cst, zst)


def _join_chunks(ref):
    return jnp.concatenate([ref[c] for c in range(ref.shape[0])], axis=1)


def _merged_block(bc, pooled, sgc, sgp, pscale, wco, wpg):
    y_conv = _dot(bc, wco)
    y_pool = _dot(pooled, wpg) * pscale
    return (sgc * y_conv + sgp * y_pool).astype(BF16)


def _outproj_body(bc_ref, pooled_ref, sgc_ref, sgp_ref, h_ref, wco_ref, wpg_ref, ps_ref, wo_ref,
                  h2_ref, wco_c, wpg_c, wo_c):
    @pl.when(pl.program_id(1) == 0)
    def _():
        h2_ref[...] = h_ref[...]

    m = _merged_block(_join_chunks(bc_ref), _join_chunks(pooled_ref), sgc_ref[0], sgp_ref[0], ps_ref[...],
                      _weight(wco_ref, wco_c), _weight(wpg_ref, wpg_c))
    h2_ref[...] += _dot(m, _weight(wo_ref, wo_c))


def _outproj(bc, pooled, sgc, sgp, h, wco, wpg, pscale, wo, *, tn, name):
    bc_chunks, rows, bc_width = bc.shape
    d = bc_chunks * bc_width
    n_groups, pool_in, group_out = wpg.shape
    n_n = d // tn
    per_group = group_out // tn
    pool_chunks = pooled.shape[0] // n_groups
    assert d == n_n * tn and group_out == per_group * tn and d == n_groups * group_out
    assert sgc.shape == sgp.shape == (n_n, rows, tn) and pool_chunks * pooled.shape[2] == pool_in
    chunk_spec = lambda r: pl.BlockSpec((1, r, tn), lambda i, n: (n // per_group, 0, n % per_group))
    w_specs = [pl.BlockSpec((d, tn), lambda i, n: (0, n)), chunk_spec(pool_in),
               pl.BlockSpec((tn, d), lambda i, n: (n, 0))]
    w_shapes = [jax.ShapeDtypeStruct((n_groups, d, group_out), BF16),
                jax.ShapeDtypeStruct(wpg.shape, BF16), jax.ShapeDtypeStruct(wo.shape, BF16)]
    w_out_specs = [chunk_spec(d), w_specs[1], w_specs[2]]
    in_specs = [
        _resident((bc_chunks, rows, bc_width), lambda i, n: (0, 0, 0)),
        pl.BlockSpec((pool_chunks, rows, pooled.shape[2]), lambda i, n: (n // per_group, 0, 0)),
        pl.BlockSpec((1, rows, tn), lambda i, n: (n, 0, 0)),
        pl.BlockSpec((1, rows, tn), lambda i, n: (n, 0, 0)),
        _resident((rows, d), lambda i, n: (0, 0)),
        w_specs[0], w_specs[1],
        pl.BlockSpec((1, tn), lambda i, n: (0, n)),
        w_specs[2],
    ]
    return pl.pallas_call(
        _outproj_body,
        grid=(1, n_n),
        in_specs=in_specs,
        out_specs=[_resident((rows, d), lambda i, n: (0, 0))] + w_out_specs,
        out_shape=[jax.ShapeDtypeStruct((rows, d), F32)] + w_shapes,
        compiler_params=_params(("arbitrary", "arbitrary")),
        name=name,
    )(bc, pooled, sgc, sgp, h, wco, wpg, pscale, wo)


def _outproj_resident_body(bc_ref, pooled_ref, sgc_ref, sgp_ref, h_ref, wco_ref, wpg_ref, ps_ref, wo_ref,
                           h2_ref, *, groups):
    n = pl.program_id(1)
    tn = sgc_ref.shape[2]

    @pl.when(n == 0)
    def _():
        h2_ref[...] = h_ref[...]

    bc = _join_chunks(bc_ref)
    total = None
    for q in range(groups):
        k = n * groups + q
        m = _merged_block(bc, pooled_ref[q], sgc_ref[q], sgp_ref[q], ps_ref[:, q * tn:(q + 1) * tn],
                          wco_ref[k], wpg_ref[k])
        part = _dot(m, wo_ref[k])
        total = part if total is None else total + part
    h2_ref[...] += total


def _outproj_resident(bc, pooled, sgc, sgp, h, wco, wpg, pscale, wo, *, tm, groups, name):
    n_groups, rows, tn = sgc.shape
    d = n_groups * tn
    n_i, n_n = rows // tm, n_groups // groups
    assert rows == n_i * tm and n_groups == n_n * groups
    assert bc.shape == sgp.shape == (n_groups, rows, tn) and pooled.shape[:2] == (n_groups, rows)
    assert wco.shape == (n_groups, d, tn) and wo.shape == (n_groups, tn, d) and wpg.shape[0] == n_groups
    whole = lambda a: _resident(a.shape, lambda i, n: (0, 0, 0))
    per_step = lambda w: pl.BlockSpec((groups, tm, w), lambda i, n: (n, i, 0))
    in_specs = [
        pl.BlockSpec((n_groups, tm, tn), lambda i, n: (0, i, 0)),
        per_step(pooled.shape[2]), per_step(tn), per_step(tn),
        pl.BlockSpec((tm, d), lambda i, n: (i, 0)),
        whole(wco), whole(wpg),
        pl.BlockSpec((1, groups * tn), lambda i, n: (0, n)),
        whole(wo),
    ]
    return pl.pallas_call(
        functools.partial(_outproj_resident_body, groups=groups),
        grid=(n_i, n_n),
        in_specs=in_specs,
        out_specs=pl.BlockSpec((tm, d), lambda i, n: (i, 0)),
        out_shape=jax.ShapeDtypeStruct((rows, d), F32),
        compiler_params=_params(("arbitrary", "arbitrary")),
        name=name,
    )(bc, pooled, sgc, sgp, h, wco, wpg, pscale, wo)


def kernel(x_prompt, x_sample, state_conv, state_pool, meta_tokens, norm_ffn1, w_ffn1_gate,
           w_ffn1_up, w_ffn1_down, norm_mix, w_in, conv_w, w_conv_out, w_pool_group, pool_scale,
           w_o, norm_ffn2, w_ffn2_gate, w_ffn2_up, w_ffn2_down, norm_final):
    batch, seq, d = x_prompt.shape
    dec_batch, dec_seq, _ = x_sample.shape
    depth = w_in.shape[0]
    d_pool = state_pool.shape[-1]
    assert depth == 1 and dec_seq == F32_SUBLANES and meta_tokens.shape[0] == N_META == MAX_WIN
    assert state_conv.shape[2] == CONV_W - 1 and state_pool.shape[2] == MAX_WIN - 1
    n_s = dec_batch * dec_seq

    row = lambda v: v.reshape(1, d)
    g1, gm, g2, gf = row(norm_ffn1[0]), row(norm_mix[0]), row(norm_ffn2[0]), row(norm_final)
    pscale = row(pool_scale[0])
    cw = conv_w[0]

    xs = jnp.concatenate([x_sample.reshape(n_s, d), meta_tokens.astype(F32)], axis=0)
    h_s, u_s, wg1, wu1, wd1 = _ffn(xs, g1, w_ffn1_gate[0], w_ffn1_up[0], w_ffn1_down[0], gm,
                                   tm=xs.shape[0], tf=256, emit_h=True, name="ffn1_sample", w_chunk=512)
    cst = jnp.pad(state_conv[0], ((0, 0), (F32_SUBLANES - (CONV_W - 1), 0), (0, 0)))
    zst = jnp.pad(state_pool[0], ((0, 0), (1, 0), (0, 0)))
    bc_s, pooled_s, sgc_s, sgp_s, cv_all, z_all, *w_groups = _mix_sample(u_s, w_in[0], cw, cst, zst, n_j=8)
    h2_s, wco, wpg, wo = _outproj(bc_s, pooled_s, sgc_s, sgp_s, h_s, w_conv_out[0], w_pool_group[0],
                                  pscale, w_o[0], tn=256, name="outproj_sample")
    y_s, wg2, wu2, wd2 = _ffn(h2_s, g2, w_ffn2_gate[0], w_ffn2_up[0], w_ffn2_down[0], gf,
                              tm=n_s, tf=256, emit_h=False, name="ffn2_sample", w_chunk=512)

    cv_s = cv_all[:n_s].reshape(dec_batch, dec_seq, d)
    new_conv_sample = cv_s[:, dec_seq - (CONV_W - 1):]
    new_pool_sample = jnp.concatenate(
        [state_pool[0][:, dec_seq:], z_all[:n_s].reshape(dec_batch, dec_seq, d_pool)], axis=1)

    cinit = jnp.pad(cv_all[n_s + N_META - (CONV_W - 1):], ((F32_SUBLANES - (CONV_W - 1), 0), (0, 0)))
    zinit = z_all[n_s:]
    xp = x_prompt.reshape(batch * seq, d)
    down_chunks = lambda wd, wg: wd.reshape(wg.shape[0], wg.shape[2], d)
    h_p, u_p = _ffn_stream(xp, g1, wg1, wu1, down_chunks(wd1, wg1), gm, tm=512, unit=2, emit_h=True,
                           name="ffn1_prompt")
    bc_p, pooled_p, sgc_p, sgp_p, cvlast, zlast = _mix_prompt(u_p, w_groups, cw, cinit, zinit,
                                                              n_seq=batch, tm=1024, n_j=4)
    tn = wpg.shape[2]
    h2_p = _outproj_resident(bc_p, pooled_p, sgc_p, sgp_p, h_p, wco, wpg, pscale, wo.reshape(d // tn, tn, d),
                             tm=512, groups=2, name="outproj_prompt")
    (y_p,) = _ffn_stream(h2_p, g2, wg2, wu2, down_chunks(wd2, wg2), gf, tm=512, unit=2, emit_h=False,
                         name="ffn2_prompt")

    return (y_p.reshape(batch, seq, d),
            y_s.reshape(dec_batch, dec_seq, d),
            cvlast[None],
            zlast[:, 1:][None],
            new_conv_sample[None],
            new_pool_sample[None])
```

```python
import functools

import jax
import jax.numpy as jnp
from jax import lax
from jax.experimental import pallas as pl
from jax.experimental.pallas import tpu as pltpu

F32 = jnp.float32
BF16 = jnp.bfloat16

EPS = 1e-6
N_META = 16
CONV_W = 3
POOL_WINDOWS = (2, 4, 8, 16)
MAX_WIN = max(POOL_WINDOWS)
N_GROUPS = len(POOL_WINDOWS)

F32_SUBLANES = 8
V7X_VMEM_LIMIT_BYTES = 60 * 1024 * 1024


def _params(semantics):
    return pltpu.CompilerParams(dimension_semantics=semantics,
                                vmem_limit_bytes=V7X_VMEM_LIMIT_BYTES)


def _dot(a, b):
    return jnp.dot(a, b, preferred_element_type=F32)


def _rms(x, g):
    ms = jnp.mean(x * x, axis=-1, keepdims=True)
    return (x * lax.rsqrt(ms + EPS)) * g


def _resident(shape, index_map):
    return pl.BlockSpec(shape, index_map, pipeline_mode=pl.Buffered(1))


def _block(ref):
    return ref[0] if len(ref.shape) == 3 else ref[...]


def _set_block(ref, value):
    if len(ref.shape) == 3:
        ref[0] = value
    else:
        ref[...] = value


def _weight(w_ref, copy_ref):
    w = _block(w_ref).astype(BF16)
    if copy_ref is not None:
        _set_block(copy_ref, w)
    return w


def _ffn_cast_body(*refs, n_parts, emit_h):
    refs = list(refs)
    x_refs = [refs.pop(0) for _ in range(n_parts)]
    g_ref, wg_ref, wu_ref, wd_ref, g2_ref, acc_ref = (refs.pop(0) for _ in range(6))
    n_ref = refs.pop(0) if emit_h else acc_ref
    wg_c, wu_c, wd_c, xn_ref = refs
    f = pl.program_id(0)

    def x():
        return jnp.concatenate([r[...] for r in x_refs], axis=0)

    @pl.when(f == 0)
    def _():
        xn_ref[...] = _rms(x(), g_ref[...]).astype(BF16)
        acc_ref[...] = jnp.zeros(acc_ref.shape, F32)

    xn = xn_ref[...]
    a = _dot(xn, _weight(wg_ref, wg_c))
    b = _dot(xn, _weight(wu_ref, wu_c))
    act = ((a * jax.nn.sigmoid(a)) * b).astype(BF16)
    acc_ref[...] += _dot(act, _weight(wd_ref, wd_c))

    @pl.when(f == pl.num_programs(0) - 1)
    def _():
        h = x() + 0.5 * acc_ref[...]
        if emit_h:
            acc_ref[...] = h
        n_ref[...] = _rms(h, g2_ref[...]).astype(n_ref.dtype)


def _ffn_cast(x_parts, g, wg, wu, wd, g2, *, tf, w_chunk, emit_h, name):
    d, d_ff = wg.shape
    rows = sum(p.shape[0] for p in x_parts)
    n_f = d_ff // tf
    assert d_ff == n_f * tf and w_chunk % tf == 0 and all(p.shape[0] % F32_SUBLANES == 0 for p in x_parts)
    per_chunk = w_chunk // tf
    whole = lambda shape: _resident(shape, lambda f: (0, 0))
    col_spec = pl.BlockSpec((d, tf), lambda f: (0, f))
    wd_spec = pl.BlockSpec((tf, d), lambda f: (f, 0))
    chunk_spec = pl.BlockSpec((1, d, tf), lambda f: (f // per_chunk, 0, f % per_chunk))
    gain_spec = pl.BlockSpec((1, d), lambda f: (0, 0))
    in_specs = [whole(p.shape) for p in x_parts] + [gain_spec, col_spec, col_spec, wd_spec, gain_spec]
    out_shape = [jax.ShapeDtypeStruct((rows, d), F32)]
    out_specs = [whole((rows, d))]
    if emit_h:
        out_shape.append(jax.ShapeDtypeStruct((rows, d), BF16))
        out_specs.append(whole((rows, d)))
    chunked = jax.ShapeDtypeStruct((d_ff // w_chunk, d, w_chunk), BF16)
    return pl.pallas_call(
        functools.partial(_ffn_cast_body, n_parts=len(x_parts), emit_h=emit_h),
        grid=(n_f,),
        in_specs=in_specs,
        out_specs=out_specs + [chunk_spec, chunk_spec, wd_spec],
        out_shape=out_shape + [chunked, chunked, jax.ShapeDtypeStruct(wd.shape, BF16)],
        scratch_shapes=[pltpu.VMEM((rows, d), BF16)],
        compiler_params=_params(("arbitrary",)),
        name=name,
    )(*x_parts, g, wg, wu, wd, g2)


W_SLOTS = 2


def _ffn_stream_body(x_ref, g_ref, wg_hbm, wu_hbm, wd_hbm, g2_ref, *refs, emit_h, units):
    refs = list(refs)
    acc_ref = refs.pop(0)
    n_ref = refs.pop(0) if emit_h else acc_ref
    xn_ref, wg_buf, wu_buf, wd_buf, sem = refs
    i = pl.program_id(0)
    n_units = len(units)

    def slot_of(u):
        return u % W_SLOTS if n_units % W_SLOTS == 0 else (i * n_units + u) % W_SLOTS

    def copy(w_hbm, w_buf, which, u):
        first, count = units[u % n_units]
        slot = slot_of(u)
        return pltpu.make_async_copy(w_hbm.at[pl.ds(first, count)], w_buf.at[slot, pl.ds(0, count)],
                                     sem.at[which, slot])

    def gate_up(u):
        return (copy(wg_hbm, wg_buf, 0, u), copy(wu_hbm, wu_buf, 1, u))

    def down(u):
        return (copy(wd_hbm, wd_buf, 2, u),)

    @pl.when(i == 0)
    def _():
        for cp in gate_up(0) + gate_up(1) + down(0):
            cp.start()
        for cp in gate_up(0):
            cp.wait()

    xn_ref[...] = _rms(x_ref[...], g_ref[...]).astype(BF16)
    for u, (_, count) in enumerate(units):
        slot = slot_of(u)
        xn = xn_ref[...]
        acts = []
        for q in range(count):
            a = _dot(xn, wg_buf[slot, q])
            b = _dot(xn, wu_buf[slot, q])
            acts.append(((a * jax.nn.sigmoid(a)) * b).astype(BF16))
        for cp in down(u) + gate_up(u + 1):
            cp.wait()
        for cp in gate_up(u + 2) + down(u + 1):
            cp.start()
        part = _dot(acts[0], wd_buf[slot, 0])
        for q in range(1, count):
            part = part + _dot(acts[q], wd_buf[slot, q])
        if u == 0:
            acc_ref[...] = part
        else:
            acc_ref[...] += part

    h = x_ref[...] + 0.5 * acc_ref[...]
    if emit_h:
        acc_ref[...] = h
    n_ref[...] = _rms(h, g2_ref[...]).astype(n_ref.dtype)

    @pl.when(i == pl.num_programs(0) - 1)
    def _():
        for cp in gate_up(n_units + 1) + down(n_units):
            cp.wait()


def _ffn_stream(x, g, wg, wu, wd, g2, *, tm, unit, emit_h, name):
    rows, d = x.shape
    n_f, _, tf = wg.shape
    n_i = rows // tm
    assert rows == n_i * tm and wu.shape == wg.shape and wd.shape == (n_f, tf, d)
    units = tuple((first, min(unit, n_f - first)) for first in range(0, n_f, unit))
    assert len(units) >= W_SLOTS
    row = lambda i: (i, 0)
    hbm = pl.BlockSpec(memory_space=pl.ANY)
    in_specs = [pl.BlockSpec((tm, d), row), pl.BlockSpec((1, d), lambda i: (0, 0)), hbm, hbm, hbm,
                pl.BlockSpec((1, d), lambda i: (0, 0))]
    out_shape = [jax.ShapeDtypeStruct((rows, d), F32)]
    out_specs = [pl.BlockSpec((tm, d), row)]
    if emit_h:
        out_shape.append(jax.ShapeDtypeStruct((rows, d), BF16))
        out_specs.append(pl.BlockSpec((tm, d), row))
    scratch = [pltpu.VMEM((tm, d), BF16),
               pltpu.VMEM((W_SLOTS, unit, d, tf), BF16), pltpu.VMEM((W_SLOTS, unit, d, tf), BF16),
               pltpu.VMEM((W_SLOTS, unit, tf, d), BF16),
               pltpu.SemaphoreType.DMA((3, W_SLOTS))]
    return pl.pallas_call(
        functools.partial(_ffn_stream_body, emit_h=emit_h, units=units),
        grid=(n_i,),
        in_specs=in_specs,
        out_specs=out_specs,
        out_shape=out_shape,
        scratch_shapes=scratch,
        compiler_params=_params(("arbitrary",)),
        name=name,
    )(x, g, wg, wu, wd, g2)


def _conv3(w_ref, cur, prev1, prev2):
    w = w_ref[...]
    y = w[0:1] * prev2
    y = y + w[1:2] * prev1
    return y + w[2:3] * cur


def _window_sums(e, g):
    p = e + pltpu.roll(e, 1, axis=0)
    for level, shift in ((1, 2), (2, 4), (3, 8)):
        p = jnp.where(g >= level, p + pltpu.roll(p, shift, axis=0), p)
    return p


def _inv_window(g):
    return jnp.where(g == 0, 0.5, jnp.where(g == 1, 0.25, jnp.where(g == 2, 0.125, 0.0625))).astype(F32)


def _mix_prompt_body(u_ref, wb_ref, wc_ref, wv_ref, wz_ref, wgc_ref, wgp_ref, cw_ref, cinit_ref,
                     zinit_ref, bc_ref, pooled_ref, sgc_ref, sgp_ref, cvlast_ref, zlast_ref,
                     cpast_ref, zpast_ref, *, tm, tiles_per_seq, n_j):
    j = pl.program_id(0)
    i = pl.program_id(1)
    t = i % tiles_per_seq
    b = i // tiles_per_seq
    g = j // (n_j // N_GROUPS)
    u = u_ref[...]

    @pl.when(t == 0)
    def _():
        cpast_ref[...] = cinit_ref[...]
        zpast_ref[...] = zinit_ref[...]

    cv = _dot(u, wc_ref[...]) * _dot(u, wv_ref[...])
    ext = jnp.concatenate([cpast_ref[...], cv], axis=0)
    y = _conv3(cw_ref, cv, pltpu.roll(ext, 1, axis=0)[F32_SUBLANES:], pltpu.roll(ext, 2, axis=0)[F32_SUBLANES:])
    bc_ref[0] = (_dot(u, wb_ref[...]) * y).astype(BF16)
    cpast_ref[...] = cv[tm - F32_SUBLANES:]

    z = _dot(u, wz_ref[...])
    sums = _window_sums(jnp.concatenate([zpast_ref[...], z], axis=0), g)[MAX_WIN:]
    pooled_ref[0] = (sums * _inv_window(g) - z).astype(BF16)
    zpast_ref[...] = z[tm - MAX_WIN:]

    sgc_ref[0] = jax.nn.sigmoid(_dot(u, wgc_ref[...]))
    sgp_ref[0] = jax.nn.sigmoid(_dot(u, wgp_ref[...]))

    @pl.when(t == tiles_per_seq - 1)
    def _():
        cvlast_ref[b] = cpast_ref[pl.ds(F32_SUBLANES - (CONV_W - 1), CONV_W - 1), :]
        zlast_ref[b] = zpast_ref[...]


def _mix_prompt(u, w_groups, conv_w, cinit, zinit, *, n_seq, tm, n_j):
    rows, d = u.shape
    d_pool = zinit.shape[1]
    tc, tz = d // n_j, d_pool // n_j
    seq = rows // n_seq
    tiles_per_seq = seq // tm
    n_i = rows // tm
    assert seq == tiles_per_seq * tm and n_j % N_GROUPS == 0 and tm >= MAX_WIN
    blk = lambda w: pl.BlockSpec((1, tm, w), lambda j, i: (j, i, 0))
    wcol = lambda w: pl.BlockSpec((d, w), lambda j, i: (0, j))
    in_specs = ([pl.BlockSpec((tm, d), lambda j, i: (i, 0))]
                + [wcol(tc), wcol(tc), wcol(tc), wcol(tz), wcol(tc), wcol(tc)]
                + [pl.BlockSpec((CONV_W, tc), lambda j, i: (0, j)),
                   pl.BlockSpec((F32_SUBLANES, tc), lambda j, i: (0, j)),
                   pl.BlockSpec((MAX_WIN, tz), lambda j, i: (0, j))])
    out_shape = (jax.ShapeDtypeStruct((n_j, rows, tc), BF16),
                 jax.ShapeDtypeStruct((n_j, rows, tz), BF16),
                 jax.ShapeDtypeStruct((n_j, rows, tc), F32),
                 jax.ShapeDtypeStruct((n_j, rows, tc), F32),
                 jax.ShapeDtypeStruct((n_seq, CONV_W - 1, d), F32),
                 jax.ShapeDtypeStruct((n_seq, MAX_WIN, d_pool), F32))
    out_specs = (blk(tc), blk(tz), blk(tc), blk(tc),
                 pl.BlockSpec((n_seq, CONV_W - 1, tc), lambda j, i: (0, 0, j)),
                 pl.BlockSpec((n_seq, MAX_WIN, tz), lambda j, i: (0, 0, j)))
    scratch = [pltpu.VMEM((F32_SUBLANES, tc), F32), pltpu.VMEM((MAX_WIN, tz), F32)]
    return pl.pallas_call(
        functools.partial(_mix_prompt_body, tm=tm, tiles_per_seq=tiles_per_seq, n_j=n_j),
        grid=(n_j, n_i),
        in_specs=in_specs,
        out_specs=out_specs,
        out_shape=out_shape,
        scratch_shapes=scratch,
        compiler_params=_params(("arbitrary", "arbitrary")),
        name="mix_prompt",
    )(u, *w_groups, conv_w, cinit, zinit)


def _mix_sample_body(u_ref, wb_ref, wc_ref, wv_ref, wz_ref, wgc_ref, wgp_ref, cw_ref, cst_ref,
                     zst_ref, bc_ref, pooled_ref, sgc_ref, sgp_ref, cnew_ref, znew_ref, cmeta_ref, zmeta_ref,
                     wb_c, wc_c, wv_c, wz_c, wgc_c, wgp_c, *, n_seq, n_j):
    seq = F32_SUBLANES
    ns = n_seq * seq
    j = pl.program_id(0)
    g = j // (n_j // N_GROUPS)
    u = u_ref[...]
    us = u[:ns]
    tc = cmeta_ref.shape[1]
    tz = zmeta_ref.shape[1]

    cv_all = _dot(u, _weight(wc_ref, wc_c)) * _dot(u, _weight(wv_ref, wv_c))
    cmeta_ref[...] = cv_all[ns:]
    cv = cv_all[:ns]
    t = lax.broadcasted_iota(jnp.int32, (ns, 1), 0) % seq
    state = cst_ref[...]
    per_token = lambda s: jnp.broadcast_to(s, (n_seq, seq, tc)).reshape(ns, tc)
    older, newer = per_token(state[:, 0:1, :]), per_token(state[:, 1:2, :])
    prev1 = jnp.where(t == 0, newer, pltpu.roll(cv, 1, axis=0))
    prev2 = jnp.where(t == 0, older, jnp.where(t == 1, newer, pltpu.roll(cv, 2, axis=0)))
    y = _conv3(cw_ref, cv, prev1, prev2)
    bc_ref[0] = (_dot(us, _weight(wb_ref, wb_c)) * y).astype(BF16)
    keep = CONV_W - 1
    cnew_ref[...] = pltpu.roll(cv, ns - (seq - keep), axis=0).reshape(n_seq, seq, tc)[:, :keep, :]

    z_all = _dot(u, _weight(wz_ref, wz_c))
    zmeta_ref[...] = z_all[ns:]
    z = z_all[:ns]
    per = MAX_WIN + seq
    ext = jnp.concatenate([zst_ref[...], z.reshape(n_seq, seq, tz)], axis=1).reshape(n_seq * per, tz)
    keep = MAX_WIN - 1
    znew_ref[...] = pltpu.roll(ext, n_seq * per - (per - keep), axis=0).reshape(n_seq, per, tz)[:, :keep, :]
    sums = _window_sums(ext, g)
    sums = sums.reshape(n_seq, per, tz)[:, MAX_WIN:, :].reshape(ns, tz)
    pooled_ref[0] = (sums * _inv_window(g) - z).astype(BF16)

    sgc_ref[0] = jax.nn.sigmoid(_dot(us, _weight(wgc_ref, wgc_c)))
    sgp_ref[0] = jax.nn.sigmoid(_dot(us, _weight(wgp_ref, wgp_c)))


def _mix_sample(u, w_in, conv_w, cst, zst, *, n_j):
    rows, d = u.shape
    n_seq, _, d_pool = zst.shape
    ns = n_seq * F32_SUBLANES
    tc, tz = d // n_j, d_pool // n_j
    assert n_j % N_GROUPS == 0 and cst.shape == (n_seq, CONV_W - 1, d) and CONV_W == 3
    assert (3 * d + d_pool) % tc == 0
    conv_blocks = d // tc
    z0 = (3 * d) // tz
    gc0 = (3 * d + d_pool) // tc
    wcol = lambda w, off: pl.BlockSpec((d, w), lambda j: (0, off + j))
    w_specs = [wcol(tc, 0), wcol(tc, conv_blocks), wcol(tc, 2 * conv_blocks), wcol(tz, z0),
               wcol(tc, gc0), wcol(tc, gc0 + conv_blocks)]
    col = lambda r, w: pl.BlockSpec((r, w), lambda j: (0, j))
    in_specs = ([_resident((rows, d), lambda j: (0, 0))] + w_specs
                + [col(CONV_W, tc),
                   pl.BlockSpec((n_seq, CONV_W - 1, tc), lambda j: (0, 0, j)),
                   pl.BlockSpec((n_seq, MAX_WIN, tz), lambda j: (0, 0, j))])
    assert rows - ns == N_META
    group_widths = (d, d, d, d_pool, d, d)
    out_shape = ([jax.ShapeDtypeStruct((n_j, ns, tc), BF16),
                  jax.ShapeDtypeStruct((n_j, ns, tz), BF16),
                  jax.ShapeDtypeStruct((n_j, ns, tc), F32),
                  jax.ShapeDtypeStruct((n_j, ns, tc), F32),
                  jax.ShapeDtypeStruct((n_seq, CONV_W - 1, d), F32),
                  jax.ShapeDtypeStruct((n_seq, MAX_WIN - 1, d_pool), F32),
                  jax.ShapeDtypeStruct((rows - ns, d), F32),
                  jax.ShapeDtypeStruct((rows - ns, d_pool), F32)]
                 + [jax.ShapeDtypeStruct((d, w), BF16) for w in group_widths])
    chunk = lambda w: pl.BlockSpec((1, ns, w), lambda j: (j, 0, 0))
    per_seq = lambda r, w: pl.BlockSpec((n_seq, r, w), lambda j: (0, 0, j))
    out_specs = ([chunk(tc), chunk(tz), chunk(tc), chunk(tc),
                  per_seq(CONV_W - 1, tc), per_seq(MAX_WIN - 1, tz), col(rows - ns, tc), col(rows - ns, tz)]
                 + [col(d, tz if w == d_pool else tc) for w in group_widths])
    return pl.pallas_call(
        functools.partial(_mix_sample_body, n_seq=n_seq, n_j=n_j),
        grid=(n_j,),
        in_specs=in_specs,
        out_specs=out_specs,
        out_shape=out_shape,
        compiler_params=_params(("arbitrary",)),
        name="mix_sample",
    )(u, w_in, w_in, w_in, w_in, w_in, w_in, conv_w, cst, zst)


def _join_chunks(ref):
    return jnp.concatenate([ref[c] for c in range(ref.shape[0])], axis=1)


def _merged_block(bc, pooled, sgc, sgp, pscale, wco, wpg):
    y_conv = _dot(bc, wco)
    y_pool = _dot(pooled, wpg) * pscale
    return (sgc * y_conv + sgp * y_pool).astype(BF16)


def _outproj_body(bc_ref, pooled_ref, sgc_ref, sgp_ref, h_ref, wco_ref, wpg_ref, ps_ref, wo_ref,
                  h2_ref, wco_c, wpg_c, wo_c):
    @pl.when(pl.program_id(1) == 0)
    def _():
        h2_ref[...] = h_ref[...]

    m = _merged_block(_join_chunks(bc_ref), _join_chunks(pooled_ref), sgc_ref[0], sgp_ref[0], ps_ref[...],
                      _weight(wco_ref, wco_c), _weight(wpg_ref, wpg_c))
    h2_ref[...] += _dot(m, _weight(wo_ref, wo_c))


def _outproj(bc, pooled, sgc, sgp, h, wco, wpg, pscale, wo, *, tn, name):
    bc_chunks, rows, bc_width = bc.shape
    d = bc_chunks * bc_width
    n_groups, pool_in, group_out = wpg.shape
    n_n = d // tn
    per_group = group_out // tn
    pool_chunks = pooled.shape[0] // n_groups
    assert d == n_n * tn and group_out == per_group * tn and d == n_groups * group_out
    assert sgc.shape == sgp.shape == (n_n, rows, tn) and pool_chunks * pooled.shape[2] == pool_in
    chunk_spec = lambda r: pl.BlockSpec((1, r, tn), lambda i, n: (n // per_group, 0, n % per_group))
    w_specs = [pl.BlockSpec((d, tn), lambda i, n: (0, n)), chunk_spec(pool_in),
               pl.BlockSpec((tn, d), lambda i, n: (n, 0))]
    w_shapes = [jax.ShapeDtypeStruct((n_groups, d, group_out), BF16),
                jax.ShapeDtypeStruct(wpg.shape, BF16), jax.ShapeDtypeStruct(wo.shape, BF16)]
    w_out_specs = [chunk_spec(d), w_specs[1], w_specs[2]]
    in_specs = [
        _resident((bc_chunks, rows, bc_width), lambda i, n: (0, 0, 0)),
        pl.BlockSpec((pool_chunks, rows, pooled.shape[2]), lambda i, n: (n // per_group, 0, 0)),
        pl.BlockSpec((1, rows, tn), lambda i, n: (n, 0, 0)),
        pl.BlockSpec((1, rows, tn), lambda i, n: (n, 0, 0)),
        _resident((rows, d), lambda i, n: (0, 0)),
        w_specs[0], w_specs[1],
        pl.BlockSpec((1, tn), lambda i, n: (0, n)),
        w_specs[2],
    ]
    return pl.pallas_call(
        _outproj_body,
        grid=(1, n_n),
        in_specs=in_specs,
        out_specs=[_resident((rows, d), lambda i, n: (0, 0))] + w_out_specs,
        out_shape=[jax.ShapeDtypeStruct((rows, d), F32)] + w_shapes,
        compiler_params=_params(("arbitrary", "arbitrary")),
        name=name,
    )(bc, pooled, sgc, sgp, h, wco, wpg, pscale, wo)


def _outproj_resident_body(bc_ref, pooled_ref, sgc_ref, sgp_ref, h_ref, wco_ref, wpg_ref, ps_ref, wo_ref,
                           h2_ref, *, groups):
    n = pl.program_id(1)
    tn = sgc_ref.shape[2]

    @pl.when(n == 0)
    def _():
        h2_ref[...] = h_ref[...]

    bc = _join_chunks(bc_ref)
    total = None
    for q in range(groups):
        k = n * groups + q
        m = _merged_block(bc, pooled_ref[q], sgc_ref[q], sgp_ref[q], ps_ref[:, q * tn:(q + 1) * tn],
                          wco_ref[k], wpg_ref[k])
        part = _dot(m, wo_ref[k])
        total = part if total is None else total + part
    h2_ref[...] += total


def _outproj_resident(bc, pooled, sgc, sgp, h, wco, wpg, pscale, wo, *, tm, groups, name):
    n_groups, rows, tn = sgc.shape
    d = n_groups * tn
    n_i, n_n = rows // tm, n_groups // groups
    assert rows == n_i * tm and n_groups == n_n * groups
    assert bc.shape == sgp.shape == (n_groups, rows, tn) and pooled.shape[:2] == (n_groups, rows)
    assert wco.shape == (n_groups, d, tn) and wo.shape == (n_groups, tn, d) and wpg.shape[0] == n_groups
    whole = lambda a: _resident(a.shape, lambda i, n: (0, 0, 0))
    per_step = lambda w: pl.BlockSpec((groups, tm, w), lambda i, n: (n, i, 0))
    in_specs = [
        pl.BlockSpec((n_groups, tm, tn), lambda i, n: (0, i, 0)),
        per_step(pooled.shape[2]), per_step(tn), per_step(tn),
        pl.BlockSpec((tm, d), lambda i, n: (i, 0)),
        whole(wco), whole(wpg),
        pl.BlockSpec((1, groups * tn), lambda i, n: (0, n)),
        whole(wo),
    ]
    return pl.pallas_call(
        functools.partial(_outproj_resident_body, groups=groups),
        grid=(n_i, n_n),
        in_specs=in_specs,
        out_specs=pl.BlockSpec((tm, d), lambda i, n: (i, 0)),
        out_shape=jax.ShapeDtypeStruct((rows, d), F32),
        compiler_params=_params(("arbitrary", "arbitrary")),
        name=name,
    )(bc, pooled, sgc, sgp, h, wco, wpg, pscale, wo)


def kernel(x_prompt, x_sample, state_conv, state_pool, meta_tokens, norm_ffn1, w_ffn1_gate,
           w_ffn1_up, w_ffn1_down, norm_mix, w_in, conv_w, w_conv_out, w_pool_group, pool_scale,
           w_o, norm_ffn2, w_ffn2_gate, w_ffn2_up, w_ffn2_down, norm_final):
    batch, seq, d = x_prompt.shape
    dec_batch, dec_seq, _ = x_sample.shape
    depth = w_in.shape[0]
    assert depth == 1 and dec_seq == F32_SUBLANES and meta_tokens.shape[0] == N_META == MAX_WIN
    assert state_conv.shape[2] == CONV_W - 1 and state_pool.shape[2] == MAX_WIN - 1
    n_s = dec_batch * dec_seq

    row = lambda v: v.reshape(1, d)
    g1, gm, g2, gf = row(norm_ffn1[0]), row(norm_mix[0]), row(norm_ffn2[0]), row(norm_final)
    pscale = row(pool_scale[0])
    cw = conv_w[0]

    h_s, u_s, wg1, wu1, wd1 = _ffn_cast((x_sample.reshape(n_s, d), meta_tokens.astype(F32)), g1,
                                        w_ffn1_gate[0], w_ffn1_up[0], w_ffn1_down[0], gm,
                                        tf=256, w_chunk=512, emit_h=True, name="ffn1_sample")
    zst = jnp.pad(state_pool[0], ((0, 0), (1, 0), (0, 0)))
    bc_s, pooled_s, sgc_s, sgp_s, conv_s, pool_s, cv_meta, z_meta, *w_groups = _mix_sample(
        u_s, w_in[0], cw, state_conv[0], zst, n_j=8)
    h2_s, wco, wpg, wo = _outproj(bc_s, pooled_s, sgc_s, sgp_s, h_s, w_conv_out[0], w_pool_group[0],
                                  pscale, w_o[0], tn=256, name="outproj_sample")
    y_s, wg2, wu2, wd2 = _ffn_cast((h2_s,), g2, w_ffn2_gate[0], w_ffn2_up[0], w_ffn2_down[0], gf,
                                   tf=256, w_chunk=512, emit_h=False, name="ffn2_sample")

    cinit = jnp.pad(cv_meta[N_META - (CONV_W - 1):], ((F32_SUBLANES - (CONV_W - 1), 0), (0, 0)))
    xp = x_prompt.reshape(batch * seq, d)
    down_chunks = lambda wd, wg: wd.reshape(wg.shape[0], wg.shape[2], d)
    h_p, u_p = _ffn_stream(xp, g1, wg1, wu1, down_chunks(wd1, wg1), gm, tm=512, unit=2, emit_h=True,
                           name="ffn1_prompt")
    bc_p, pooled_p, sgc_p, sgp_p, cvlast, zlast = _mix_prompt(u_p, w_groups, cw, cinit, z_meta,
                                                              n_seq=batch, tm=1024, n_j=4)
    tn = wpg.shape[2]
    h2_p = _outproj_resident(bc_p, pooled_p, sgc_p, sgp_p, h_p, wco, wpg, pscale, wo.reshape(d // tn, tn, d),
                             tm=512, groups=2, name="outproj_prompt")
    (y_p,) = _ffn_stream(h2_p, g2, wg2, wu2, down_chunks(wd2, wg2), gf, tm=512, unit=2, emit_h=False,
                         name="ffn2_prompt")

    return (y_p.reshape(batch, seq, d),
            y_s.reshape(dec_batch, dec_seq, d),
            cvlast[None],
            zlast[:, 1:][None],
            conv_s[None],
            pool_s[None])
```

```python
import functools

import jax
import jax.numpy as jnp
from jax import lax
from jax.experimental import pallas as pl
from jax.experimental.pallas import tpu as pltpu

F32 = jnp.float32
BF16 = jnp.bfloat16

EPS = 1e-6
N_META = 16
CONV_W = 3
POOL_WINDOWS = (2, 4, 8, 16)
MAX_WIN = max(POOL_WINDOWS)
N_GROUPS = len(POOL_WINDOWS)

F32_SUBLANES = 8
MXU_COLUMNS = 256
V7X_VMEM_LIMIT_BYTES = 60 * 1024 * 1024


def _params(semantics):
    return pltpu.CompilerParams(dimension_semantics=semantics,
                                vmem_limit_bytes=V7X_VMEM_LIMIT_BYTES)


def _dot(a, b):
    return jnp.dot(a, b, preferred_element_type=F32)


def _rms(x, g):
    ms = jnp.mean(x * x, axis=-1, keepdims=True)
    return (x * lax.rsqrt(ms + EPS)) * g


def _resident(shape, index_map):
    return pl.BlockSpec(shape, index_map, pipeline_mode=pl.Buffered(1))


def _block(ref):
    return ref[0] if len(ref.shape) == 3 else ref[...]


def _set_block(ref, value):
    if len(ref.shape) == 3:
        ref[0] = value
    else:
        ref[...] = value


def _weight(w_ref, copy_ref):
    w = _block(w_ref).astype(BF16)
    if copy_ref is not None:
        _set_block(copy_ref, w)
    return w


def _ffn_cast_body(*refs, n_parts, emit_h):
    refs = list(refs)
    x_refs = [refs.pop(0) for _ in range(n_parts)]
    g_ref, wg_ref, wu_ref, wd_ref, g2_ref, acc_ref = (refs.pop(0) for _ in range(6))
    n_ref = refs.pop(0) if emit_h else acc_ref
    wg_c, wu_c, wd_c, xn_ref = refs
    f = pl.program_id(0)

    def x():
        return jnp.concatenate([r[...].reshape(-1, r.shape[-1]) for r in x_refs], axis=0)

    @pl.when(f == 0)
    def _():
        xn_ref[...] = _rms(x(), g_ref[...]).astype(BF16)
        acc_ref[...] = jnp.zeros(acc_ref.shape, F32)

    xn = xn_ref[...]
    a = _dot(xn, _weight(wg_ref, wg_c))
    b = _dot(xn, _weight(wu_ref, wu_c))
    act = ((a * jax.nn.sigmoid(a)) * b).astype(BF16)
    acc_ref[...] += _dot(act, _weight(wd_ref, wd_c))

    @pl.when(f == pl.num_programs(0) - 1)
    def _():
        h = x() + 0.5 * acc_ref[...]
        if emit_h:
            acc_ref[...] = h
        n_ref[...] = _rms(h, g2_ref[...]).astype(n_ref.dtype)


def _ffn_cast(x_parts, g, wg, wu, wd, g2, *, tf, w_chunk, emit_h, name):
    d, d_ff = wg.shape
    part_rows = [p.size // d for p in x_parts]
    rows = sum(part_rows)
    n_f = d_ff // tf
    assert d_ff == n_f * tf and w_chunk % tf == 0 and all(r % F32_SUBLANES == 0 for r in part_rows)
    assert all(p.ndim == 2 or p.shape[1] == F32_SUBLANES for p in x_parts)
    per_chunk = w_chunk // tf
    whole = lambda shape: _resident(shape, lambda f: (0,) * len(shape))
    col_spec = pl.BlockSpec((d, tf), lambda f: (0, f))
    wd_spec = pl.BlockSpec((tf, d), lambda f: (f, 0))
    chunk_spec = pl.BlockSpec((1, d, tf), lambda f: (f // per_chunk, 0, f % per_chunk))
    gain_spec = pl.BlockSpec((1, d), lambda f: (0, 0))
    in_specs = [whole(p.shape) for p in x_parts] + [gain_spec, col_spec, col_spec, wd_spec, gain_spec]
    out_shape = [jax.ShapeDtypeStruct((rows, d), F32)]
    out_specs = [whole((rows, d))]
    if emit_h:
        out_shape.append(jax.ShapeDtypeStruct((rows, d), BF16))
        out_specs.append(whole((rows, d)))
    chunked = jax.ShapeDtypeStruct((d_ff // w_chunk, d, w_chunk), BF16)
    return pl.pallas_call(
        functools.partial(_ffn_cast_body, n_parts=len(x_parts), emit_h=emit_h),
        grid=(n_f,),
        in_specs=in_specs,
        out_specs=out_specs + [chunk_spec, chunk_spec, wd_spec],
        out_shape=out_shape + [chunked, chunked, jax.ShapeDtypeStruct(wd.shape, BF16)],
        scratch_shapes=[pltpu.VMEM((rows, d), BF16)],
        compiler_params=_params(("arbitrary",)),
        name=name,
    )(*x_parts, g, wg, wu, wd, g2)


W_SLOTS = 2


def _ffn_stream_body(x_ref, g_ref, wg_hbm, wu_hbm, wd_hbm, g2_ref, *refs, emit_h, units):
    refs = list(refs)
    acc_ref = refs.pop(0)
    n_ref = refs.pop(0) if emit_h else acc_ref
    xn_ref, wg_buf, wu_buf, wd_buf, sem = refs
    i = pl.program_id(0)
    n_units = len(units)

    def slot_of(u):
        return u % W_SLOTS if n_units % W_SLOTS == 0 else (i * n_units + u) % W_SLOTS

    def copy(w_hbm, w_buf, which, u):
        first, count = units[u % n_units]
        slot = slot_of(u)
        return pltpu.make_async_copy(w_hbm.at[pl.ds(first, count)], w_buf.at[slot, pl.ds(0, count)],
                                     sem.at[which, slot])

    def gate_up(u):
        return (copy(wg_hbm, wg_buf, 0, u), copy(wu_hbm, wu_buf, 1, u))

    def down(u):
        return (copy(wd_hbm, wd_buf, 2, u),)

    @pl.when(i == 0)
    def _():
        for cp in gate_up(0) + gate_up(1) + down(0):
            cp.start()
        for cp in gate_up(0):
            cp.wait()

    xn_ref[...] = _rms(x_ref[...], g_ref[...]).astype(BF16)
    for u, (_, count) in enumerate(units):
        slot = slot_of(u)
        xn = xn_ref[...]
        acts = []
        for q in range(count):
            halves = []
            for c in range(0, wg_buf.shape[3], MXU_COLUMNS):
                a = _dot(xn, wg_buf[slot, q, :, c:c + MXU_COLUMNS])
                b = _dot(xn, wu_buf[slot, q, :, c:c + MXU_COLUMNS])
                halves.append(((a * jax.nn.sigmoid(a)) * b).astype(BF16))
            acts.append(jnp.concatenate(halves, axis=1))
        for cp in down(u) + gate_up(u + 1):
            cp.wait()
        for cp in gate_up(u + 2) + down(u + 1):
            cp.start()
        part = _dot(acts[0], wd_buf[slot, 0])
        for q in range(1, count):
            part = part + _dot(acts[q], wd_buf[slot, q])
        if u == 0:
            acc_ref[...] = part
        else:
            acc_ref[...] += part

    h = x_ref[...] + 0.5 * acc_ref[...]
    if emit_h:
        acc_ref[...] = h
    n_ref[...] = _rms(h, g2_ref[...]).astype(n_ref.dtype)

    @pl.when(i == pl.num_programs(0) - 1)
    def _():
        for cp in gate_up(n_units + 1) + down(n_units):
            cp.wait()


def _ffn_stream(x, g, wg, wu, wd, g2, *, tm, unit, emit_h, name):
    rows, d = x.shape
    n_f, _, tf = wg.shape
    n_i = rows // tm
    assert rows == n_i * tm and wu.shape == wg.shape and wd.shape == (n_f, tf, d)
    units = tuple((first, min(unit, n_f - first)) for first in range(0, n_f, unit))
    assert len(units) >= W_SLOTS
    row = lambda i: (i, 0)
    hbm = pl.BlockSpec(memory_space=pl.ANY)
    in_specs = [pl.BlockSpec((tm, d), row), pl.BlockSpec((1, d), lambda i: (0, 0)), hbm, hbm, hbm,
                pl.BlockSpec((1, d), lambda i: (0, 0))]
    out_shape = [jax.ShapeDtypeStruct((rows, d), F32)]
    out_specs = [pl.BlockSpec((tm, d), row)]
    if emit_h:
        out_shape.append(jax.ShapeDtypeStruct((rows, d), BF16))
        out_specs.append(pl.BlockSpec((tm, d), row))
    scratch = [pltpu.VMEM((tm, d), BF16),
               pltpu.VMEM((W_SLOTS, unit, d, tf), BF16), pltpu.VMEM((W_SLOTS, unit, d, tf), BF16),
               pltpu.VMEM((W_SLOTS, unit, tf, d), BF16),
               pltpu.SemaphoreType.DMA((3, W_SLOTS))]
    return pl.pallas_call(
        functools.partial(_ffn_stream_body, emit_h=emit_h, units=units),
        grid=(n_i,),
        in_specs=in_specs,
        out_specs=out_specs,
        out_shape=out_shape,
        scratch_shapes=scratch,
        compiler_params=_params(("arbitrary",)),
        name=name,
    )(x, g, wg, wu, wd, g2)


def _conv3(w_ref, cur, prev1, prev2):
    w = w_ref[...]
    y = w[0:1] * prev2
    y = y + w[1:2] * prev1
    return y + w[2:3] * cur


def _window_sums(e, g):
    p = e + pltpu.roll(e, 1, axis=0)
    for level, shift in ((1, 2), (2, 4), (3, 8)):
        p = jnp.where(g >= level, p + pltpu.roll(p, shift, axis=0), p)
    return p


def _inv_window(g):
    return jnp.where(g == 0, 0.5, jnp.where(g == 1, 0.25, jnp.where(g == 2, 0.125, 0.0625))).astype(F32)


def _mix_prompt_body(u_ref, wb_ref, wc_ref, wv_ref, wz_ref, wgc_ref, wgp_ref, cw_ref, cinit_ref,
                     zinit_ref, bc_ref, pooled_ref, sgc_ref, sgp_ref, cvlast_ref, zlast_ref,
                     cpast_ref, zpast_ref, *, tm, tiles_per_seq, n_j):
    j = pl.program_id(0)
    i = pl.program_id(1)
    t = i % tiles_per_seq
    b = i // tiles_per_seq
    g = j // (n_j // N_GROUPS)
    u = u_ref[...]

    @pl.when(t == 0)
    def _():
        cpast_ref[...] = cinit_ref[...]
        zpast_ref[...] = zinit_ref[...]

    cv = _dot(u, wc_ref[...]) * _dot(u, wv_ref[...])
    ext = jnp.concatenate([cpast_ref[...], cv], axis=0)
    y = _conv3(cw_ref, cv, pltpu.roll(ext, 1, axis=0)[F32_SUBLANES:], pltpu.roll(ext, 2, axis=0)[F32_SUBLANES:])
    bc_ref[0] = (_dot(u, wb_ref[...]) * y).astype(BF16)
    cpast_ref[...] = cv[tm - F32_SUBLANES:]

    z = _dot(u, wz_ref[...])
    sums = _window_sums(jnp.concatenate([zpast_ref[...], z], axis=0), g)[MAX_WIN:]
    pooled_ref[0] = (sums * _inv_window(g) - z).astype(BF16)
    zpast_ref[...] = z[tm - MAX_WIN:]

    sgc_ref[0] = jax.nn.sigmoid(_dot(u, wgc_ref[...]))
    sgp_ref[0] = jax.nn.sigmoid(_dot(u, wgp_ref[...]))

    @pl.when(t == tiles_per_seq - 1)
    def _():
        cvlast_ref[b] = cpast_ref[pl.ds(F32_SUBLANES - (CONV_W - 1), CONV_W - 1), :]
        zlast_ref[b] = zpast_ref[...]


def _mix_prompt(u, w_groups, conv_w, cinit, zinit, *, n_seq, tm, n_j):
    rows, d = u.shape
    d_pool = zinit.shape[1]
    tc, tz = d // n_j, d_pool // n_j
    seq = rows // n_seq
    tiles_per_seq = seq // tm
    n_i = rows // tm
    assert seq == tiles_per_seq * tm and n_j % N_GROUPS == 0 and tm >= MAX_WIN
    blk = lambda w: pl.BlockSpec((1, tm, w), lambda j, i: (j, i, 0))
    wcol = lambda w: pl.BlockSpec((d, w), lambda j, i: (0, j))
    in_specs = ([pl.BlockSpec((tm, d), lambda j, i: (i, 0))]
                + [wcol(tc), wcol(tc), wcol(tc), wcol(tz), wcol(tc), wcol(tc)]
                + [pl.BlockSpec((CONV_W, tc), lambda j, i: (0, j)),
                   pl.BlockSpec((F32_SUBLANES, tc), lambda j, i: (0, j)),
                   pl.BlockSpec((MAX_WIN, tz), lambda j, i: (0, j))])
    out_shape = (jax.ShapeDtypeStruct((n_j, rows, tc), BF16),
                 jax.ShapeDtypeStruct((n_j, rows, tz), BF16),
                 jax.ShapeDtypeStruct((n_j, rows, tc), F32),
                 jax.ShapeDtypeStruct((n_j, rows, tc), F32),
                 jax.ShapeDtypeStruct((n_seq, CONV_W - 1, d), F32),
                 jax.ShapeDtypeStruct((n_seq, MAX_WIN, d_pool), F32))
    out_specs = (blk(tc), blk(tz), blk(tc), blk(tc),
                 pl.BlockSpec((n_seq, CONV_W - 1, tc), lambda j, i: (0, 0, j)),
                 pl.BlockSpec((n_seq, MAX_WIN, tz), lambda j, i: (0, 0, j)))
    scratch = [pltpu.VMEM((F32_SUBLANES, tc), F32), pltpu.VMEM((MAX_WIN, tz), F32)]
    return pl.pallas_call(
        functools.partial(_mix_prompt_body, tm=tm, tiles_per_seq=tiles_per_seq, n_j=n_j),
        grid=(n_j, n_i),
        in_specs=in_specs,
        out_specs=out_specs,
        out_shape=out_shape,
        scratch_shapes=scratch,
        compiler_params=_params(("arbitrary", "arbitrary")),
        name="mix_prompt",
    )(u, *w_groups, conv_w, cinit, zinit)


def _mix_sample_body(u_ref, wb_ref, wc_ref, wv_ref, wz_ref, wgc_ref, wgp_ref, cw_ref, cst_ref,
                     zst_ref, bc_ref, pooled_ref, sgc_ref, sgp_ref, cvall_ref, zall_ref,
                     wb_c, wc_c, wv_c, wz_c, wgc_c, wgp_c, *, n_seq, n_j):
    seq = F32_SUBLANES
    ns = n_seq * seq
    j = pl.program_id(0)
    g = j // (n_j // N_GROUPS)
    u = u_ref[...]
    us = u[:ns]
    tc = cvall_ref.shape[1]
    tz = zall_ref.shape[1]

    cv_all = _dot(u, _weight(wc_ref, wc_c)) * _dot(u, _weight(wv_ref, wv_c))
    cvall_ref[...] = cv_all
    cv = cv_all[:ns]
    t = lax.broadcasted_iota(jnp.int32, (ns, 1), 0) % seq
    state = cst_ref[...]
    per_token = lambda s: jnp.broadcast_to(s, (n_seq, seq, tc)).reshape(ns, tc)
    older, newer = per_token(state[:, 0:1, :]), per_token(state[:, 1:2, :])
    prev1 = jnp.where(t == 0, newer, pltpu.roll(cv, 1, axis=0))
    prev2 = jnp.where(t == 0, older, jnp.where(t == 1, newer, pltpu.roll(cv, 2, axis=0)))
    y = _conv3(cw_ref, cv, prev1, prev2)
    bc_ref[0] = (_dot(us, _weight(wb_ref, wb_c)) * y).astype(BF16)

    z_all = _dot(u, _weight(wz_ref, wz_c))
    zall_ref[...] = z_all
    z = z_all[:ns]
    per = MAX_WIN + seq
    ext = jnp.concatenate([zst_ref[...], z.reshape(n_seq, seq, tz)], axis=1)
    sums = _window_sums(ext.reshape(n_seq * per, tz), g)
    sums = sums.reshape(n_seq, per, tz)[:, MAX_WIN:, :].reshape(ns, tz)
    pooled_ref[0] = (sums * _inv_window(g) - z).astype(BF16)

    sgc_ref[0] = jax.nn.sigmoid(_dot(us, _weight(wgc_ref, wgc_c)))
    sgp_ref[0] = jax.nn.sigmoid(_dot(us, _weight(wgp_ref, wgp_c)))


def _mix_sample(u, w_in, conv_w, cst, zst, *, n_j):
    rows, d = u.shape
    n_seq, _, d_pool = zst.shape
    ns = n_seq * F32_SUBLANES
    tc, tz = d // n_j, d_pool // n_j
    assert n_j % N_GROUPS == 0 and cst.shape == (n_seq, CONV_W - 1, d) and CONV_W == 3
    assert (3 * d + d_pool) % tc == 0
    conv_blocks = d // tc
    z0 = (3 * d) // tz
    gc0 = (3 * d + d_pool) // tc
    wcol = lambda w, off: pl.BlockSpec((d, w), lambda j: (0, off + j))
    w_specs = [wcol(tc, 0), wcol(tc, conv_blocks), wcol(tc, 2 * conv_blocks), wcol(tz, z0),
               wcol(tc, gc0), wcol(tc, gc0 + conv_blocks)]
    col = lambda r, w: pl.BlockSpec((r, w), lambda j: (0, j))
    in_specs = ([_resident((rows, d), lambda j: (0, 0))] + w_specs
                + [col(CONV_W, tc),
                   pl.BlockSpec((n_seq, CONV_W - 1, tc), lambda j: (0, 0, j)),
                   pl.BlockSpec((n_seq, MAX_WIN, tz), lambda j: (0, 0, j))])
    group_widths = (d, d, d, d_pool, d, d)
    out_shape = ([jax.ShapeDtypeStruct((n_j, ns, tc), BF16),
                  jax.ShapeDtypeStruct((n_j, ns, tz), BF16),
                  jax.ShapeDtypeStruct((n_j, ns, tc), F32),
                  jax.ShapeDtypeStruct((n_j, ns, tc), F32),
                  jax.ShapeDtypeStruct((rows, d), F32),
                  jax.ShapeDtypeStruct((rows, d_pool), F32)]
                 + [jax.ShapeDtypeStruct((d, w), BF16) for w in group_widths])
    chunk = lambda w: pl.BlockSpec((1, ns, w), lambda j: (j, 0, 0))
    out_specs = ([chunk(tc), chunk(tz), chunk(tc), chunk(tc), col(rows, tc), col(rows, tz)]
                 + [col(d, tz if w == d_pool else tc) for w in group_widths])
    return pl.pallas_call(
        functools.partial(_mix_sample_body, n_seq=n_seq, n_j=n_j),
        grid=(n_j,),
        in_specs=in_specs,
        out_specs=out_specs,
        out_shape=out_shape,
        compiler_params=_params(("arbitrary",)),
        name="mix_sample",
    )(u, w_in, w_in, w_in, w_in, w_in, w_in, conv_w, cst, zst)


def _join_chunks(ref):
    return jnp.concatenate([ref[c] for c in range(ref.shape[0])], axis=1)


def _merged_block(bc, pooled, sgc, sgp, pscale, wco, wpg):
    if wco.shape[1] <= MXU_COLUMNS:
        half = bc.shape[0] // 2
        y_conv = jnp.concatenate([_dot(bc[:half], wco), _dot(bc[half:], wco)], axis=0)
    else:
        y_conv = _dot(bc, wco)
    y_pool = _dot(pooled, wpg) * pscale
    return (sgc * y_conv + sgp * y_pool).astype(BF16)


def _outproj_body(bc_ref, pooled_ref, sgc_ref, sgp_ref, h_ref, wco_ref, wpg_ref, ps_ref, wo_ref,
                  h2_ref, wco_c, wpg_c, wo_c):
    @pl.when(pl.program_id(1) == 0)
    def _():
        h2_ref[...] = h_ref[...]

    m = _merged_block(_join_chunks(bc_ref), _join_chunks(pooled_ref), sgc_ref[0], sgp_ref[0], ps_ref[...],
                      _weight(wco_ref, wco_c), _weight(wpg_ref, wpg_c))
    h2_ref[...] += _dot(m, _weight(wo_ref, wo_c))


def _outproj(bc, pooled, sgc, sgp, h, wco, wpg, pscale, wo, *, tn, name):
    bc_chunks, rows, bc_width = bc.shape
    d = bc_chunks * bc_width
    n_groups, pool_in, group_out = wpg.shape
    n_n = d // tn
    per_group = group_out // tn
    pool_chunks = pooled.shape[0] // n_groups
    assert d == n_n * tn and group_out == per_group * tn and d == n_groups * group_out
    assert sgc.shape == sgp.shape == (n_n, rows, tn) and pool_chunks * pooled.shape[2] == pool_in
    chunk_spec = lambda r: pl.BlockSpec((1, r, tn), lambda i, n: (n // per_group, 0, n % per_group))
    w_specs = [pl.BlockSpec((d, tn), lambda i, n: (0, n)), chunk_spec(pool_in),
               pl.BlockSpec((tn, d), lambda i, n: (n, 0))]
    w_shapes = [jax.ShapeDtypeStruct((n_groups, d, group_out), BF16),
                jax.ShapeDtypeStruct(wpg.shape, BF16), jax.ShapeDtypeStruct(wo.shape, BF16)]
    w_out_specs = [chunk_spec(d), w_specs[1], w_specs[2]]
    in_specs = [
        _resident((bc_chunks, rows, bc_width), lambda i, n: (0, 0, 0)),
        pl.BlockSpec((pool_chunks, rows, pooled.shape[2]), lambda i, n: (n // per_group, 0, 0)),
        pl.BlockSpec((1, rows, tn), lambda i, n: (n, 0, 0)),
        pl.BlockSpec((1, rows, tn), lambda i, n: (n, 0, 0)),
        _resident((rows, d), lambda i, n: (0, 0)),
        w_specs[0], w_specs[1],
        pl.BlockSpec((1, tn), lambda i, n: (0, n)),
        w_specs[2],
    ]
    return pl.pallas_call(
        _outproj_body,
        grid=(1, n_n),
        in_specs=in_specs,
        out_specs=[_resident((rows, d), lambda i, n: (0, 0))] + w_out_specs,
        out_shape=[jax.ShapeDtypeStruct((rows, d), F32)] + w_shapes,
        compiler_params=_params(("arbitrary", "arbitrary")),
        name=name,
    )(bc, pooled, sgc, sgp, h, wco, wpg, pscale, wo)


def _outproj_resident_body(bc_ref, pooled_ref, sgc_ref, sgp_ref, h_ref, wco_ref, wpg_ref, ps_ref, wo_ref,
                           h2_ref, *, groups):
    n = pl.program_id(1)
    tn = sgc_ref.shape[2]

    @pl.when(n == 0)
    def _():
        h2_ref[...] = h_ref[...]

    bc = _join_chunks(bc_ref)
    total = None
    for q in range(groups):
        k = n * groups + q
        m = _merged_block(bc, pooled_ref[q], sgc_ref[q], sgp_ref[q], ps_ref[:, q * tn:(q + 1) * tn],
                          wco_ref[k], wpg_ref[k])
        part = _dot(m, wo_ref[k])
        total = part if total is None else total + part
    h2_ref[...] += total


def _outproj_resident(bc, pooled, sgc, sgp, h, wco, wpg, pscale, wo, *, tm, groups, name):
    n_groups, rows, tn = sgc.shape
    d = n_groups * tn
    n_i, n_n = rows // tm, n_groups // groups
    assert rows == n_i * tm and n_groups == n_n * groups
    assert bc.shape == sgp.shape == (n_groups, rows, tn) and pooled.shape[:2] == (n_groups, rows)
    assert wco.shape == (n_groups, d, tn) and wo.shape == (n_groups, tn, d) and wpg.shape[0] == n_groups
    whole = lambda a: _resident(a.shape, lambda i, n: (0, 0, 0))
    per_step = lambda w: pl.BlockSpec((groups, tm, w), lambda i, n: (n, i, 0))
    in_specs = [
        pl.BlockSpec((n_groups, tm, tn), lambda i, n: (0, i, 0)),
        per_step(pooled.shape[2]), per_step(tn), per_step(tn),
        pl.BlockSpec((tm, d), lambda i, n: (i, 0)),
        whole(wco), whole(wpg),
        pl.BlockSpec((1, groups * tn), lambda i, n: (0, n)),
        whole(wo),
    ]
    return pl.pallas_call(
        functools.partial(_outproj_resident_body, groups=groups),
        grid=(n_i, n_n),
        in_specs=in_specs,
        out_specs=pl.BlockSpec((tm, d), lambda i, n: (i, 0)),
        out_shape=jax.ShapeDtypeStruct((rows, d), F32),
        compiler_params=_params(("arbitrary", "arbitrary")),
        name=name,
    )(bc, pooled, sgc, sgp, h, wco, wpg, pscale, wo)


def kernel(x_prompt, x_sample, state_conv, state_pool, meta_tokens, norm_ffn1, w_ffn1_gate,
           w_ffn1_up, w_ffn1_down, norm_mix, w_in, conv_w, w_conv_out, w_pool_group, pool_scale,
           w_o, norm_ffn2, w_ffn2_gate, w_ffn2_up, w_ffn2_down, norm_final):
    batch, seq, d = x_prompt.shape
    dec_batch, dec_seq, _ = x_sample.shape
    depth = w_in.shape[0]
    d_pool = state_pool.shape[-1]
    assert depth == 1 and dec_seq == F32_SUBLANES and meta_tokens.shape[0] == N_META == MAX_WIN
    assert state_conv.shape[2] == CONV_W - 1 and state_pool.shape[2] == MAX_WIN - 1
    n_s = dec_batch * dec_seq

    row = lambda v: v.reshape(1, d)
    g1, gm, g2, gf = row(norm_ffn1[0]), row(norm_mix[0]), row(norm_ffn2[0]), row(norm_final)
    pscale = row(pool_scale[0])
    cw = conv_w[0]

    h_s, u_s, wg1, wu1, wd1 = _ffn_cast((x_sample, meta_tokens.astype(F32)), g1,
                                        w_ffn1_gate[0], w_ffn1_up[0], w_ffn1_down[0], gm,
                                        tf=256, w_chunk=512, emit_h=True, name="ffn1_sample")
    zst = jnp.pad(state_pool[0], ((0, 0), (1, 0), (0, 0)))
    bc_s, pooled_s, sgc_s, sgp_s, cv_all, z_all, *w_groups = _mix_sample(u_s, w_in[0], cw, state_conv[0], zst, n_j=8)
    h2_s, wco, wpg, wo = _outproj(bc_s, pooled_s, sgc_s, sgp_s, h_s, w_conv_out[0], w_pool_group[0],
                                  pscale, w_o[0], tn=256, name="outproj_sample")
    y_s, wg2, wu2, wd2 = _ffn_cast((h2_s,), g2, w_ffn2_gate[0], w_ffn2_up[0], w_ffn2_down[0], gf,
                                   tf=256, w_chunk=512, emit_h=False, name="ffn2_sample")

    cv_s = cv_all[:n_s].reshape(dec_batch, dec_seq, d)
    new_conv_sample = cv_s[:, dec_seq - (CONV_W - 1):]
    new_pool_sample = jnp.concatenate(
        [state_pool[0][:, dec_seq:], z_all[:n_s].reshape(dec_batch, dec_seq, d_pool)], axis=1)

    cinit = jnp.pad(cv_all[n_s + N_META - (CONV_W - 1):], ((F32_SUBLANES - (CONV_W - 1), 0), (0, 0)))
    zinit = z_all[n_s:]
    xp = x_prompt.reshape(batch * seq, d)
    down_chunks = lambda wd, wg: wd.reshape(wg.shape[0], wg.shape[2], d)
    h_p, u_p = _ffn_stream(xp, g1, wg1, wu1, down_chunks(wd1, wg1), gm, tm=512, unit=2, emit_h=True,
                           name="ffn1_prompt")
    bc_p, pooled_p, sgc_p, sgp_p, cvlast, zlast = _mix_prompt(u_p, w_groups, cw, cinit, zinit,
                                                              n_seq=batch, tm=1024, n_j=4)
    tn = wpg.shape[2]
    h2_p = _outproj_resident(bc_p, pooled_p, sgc_p, sgp_p, h_p, wco, wpg, pscale, wo.reshape(d // tn, tn, d),
                             tm=512, groups=2, name="outproj_prompt")
    (y_p,) = _ffn_stream(h2_p, g2, wg2, wu2, down_chunks(wd2, wg2), gf, tm=512, unit=2, emit_h=False,
                         name="ffn2_prompt")

    return (y_p.reshape(batch, seq, d),
            y_s.reshape(dec_batch, dec_seq, d),
            cvlast[None],
            zlast[:, 1:][None],
            new_conv_sample[None],
            new_pool_sample[None])
```

```python
import functools

import jax
import jax.numpy as jnp
from jax import lax
from jax.experimental import pallas as pl
from jax.experimental.pallas import tpu as pltpu

F32 = jnp.float32
BF16 = jnp.bfloat16

EPS = 1e-6
N_META = 16
CONV_W = 3
POOL_WINDOWS = (2, 4, 8, 16)
MAX_WIN = max(POOL_WINDOWS)
N_GROUPS = len(POOL_WINDOWS)

F32_SUBLANES = 8
MXU_COLUMNS = 256
V7X_VMEM_LIMIT_BYTES = 60 * 1024 * 1024


def _params(semantics):
    return pltpu.CompilerParams(dimension_semantics=semantics,
                                vmem_limit_bytes=V7X_VMEM_LIMIT_BYTES)


def _dot(a, b):
    return jnp.dot(a, b, preferred_element_type=F32)


def _rms(x, g):
    ms = jnp.mean(x * x, axis=-1, keepdims=True)
    return (x * lax.rsqrt(ms + EPS)) * g


def _resident(shape, index_map):
    return pl.BlockSpec(shape, index_map, pipeline_mode=pl.Buffered(1))


def _block(ref):
    return ref[0] if len(ref.shape) == 3 else ref[...]


def _set_block(ref, value):
    if len(ref.shape) == 3:
        ref[0] = value
    else:
        ref[...] = value


def _weight(w_ref, copy_ref):
    w = _block(w_ref).astype(BF16)
    if copy_ref is not None:
        _set_block(copy_ref, w)
    return w


def _ffn_cast_body(*refs, n_parts, emit_h):
    refs = list(refs)
    x_refs = [refs.pop(0) for _ in range(n_parts)]
    g_ref, wg_ref, wu_ref, wd_ref, g2_ref, acc_ref = (refs.pop(0) for _ in range(6))
    n_ref = refs.pop(0) if emit_h else acc_ref
    wg_c, wu_c, wd_c, xn_ref = refs
    f = pl.program_id(0)

    def x():
        return jnp.concatenate([r[...].reshape(-1, r.shape[-1]) for r in x_refs], axis=0)

    @pl.when(f == 0)
    def _():
        xn_ref[...] = _rms(x(), g_ref[...]).astype(BF16)
        acc_ref[...] = jnp.zeros(acc_ref.shape, F32)

    xn = xn_ref[...]
    a = _dot(xn, _weight(wg_ref, wg_c))
    b = _dot(xn, _weight(wu_ref, wu_c))
    act = ((a * jax.nn.sigmoid(a)) * b).astype(BF16)
    acc_ref[...] += _dot(act, _weight(wd_ref, wd_c))

    @pl.when(f == pl.num_programs(0) - 1)
    def _():
        h = x() + 0.5 * acc_ref[...]
        if emit_h:
            acc_ref[...] = h
        n_ref[...] = _rms(h, g2_ref[...]).astype(n_ref.dtype)


def _ffn_cast(x_parts, g, wg, wu, wd, g2, *, tf, w_chunk, emit_h, name):
    d, d_ff = wg.shape
    part_rows = [p.size // d for p in x_parts]
    rows = sum(part_rows)
    n_f = d_ff // tf
    assert d_ff == n_f * tf and w_chunk % tf == 0 and all(r % F32_SUBLANES == 0 for r in part_rows)
    assert all(p.ndim == 2 or p.shape[1] == F32_SUBLANES for p in x_parts)
    per_chunk = w_chunk // tf
    whole = lambda shape: _resident(shape, lambda f: (0,) * len(shape))
    col_spec = pl.BlockSpec((d, tf), lambda f: (0, f))
    wd_spec = pl.BlockSpec((tf, d), lambda f: (f, 0))
    chunk_spec = pl.BlockSpec((1, d, tf), lambda f: (f // per_chunk, 0, f % per_chunk))
    gain_spec = pl.BlockSpec((1, d), lambda f: (0, 0))
    in_specs = [whole(p.shape) for p in x_parts] + [gain_spec, col_spec, col_spec, wd_spec, gain_spec]
    out_shape = [jax.ShapeDtypeStruct((rows, d), F32)]
    out_specs = [whole((rows, d))]
    if emit_h:
        out_shape.append(jax.ShapeDtypeStruct((rows, d), BF16))
        out_specs.append(whole((rows, d)))
    chunked = jax.ShapeDtypeStruct((d_ff // w_chunk, d, w_chunk), BF16)
    return pl.pallas_call(
        functools.partial(_ffn_cast_body, n_parts=len(x_parts), emit_h=emit_h),
        grid=(n_f,),
        in_specs=in_specs,
        out_specs=out_specs + [chunk_spec, chunk_spec, wd_spec],
        out_shape=out_shape + [chunked, chunked, jax.ShapeDtypeStruct(wd.shape, BF16)],
        scratch_shapes=[pltpu.VMEM((rows, d), BF16)],
        compiler_params=_params(("arbitrary",)),
        name=name,
    )(*x_parts, g, wg, wu, wd, g2)


W_SLOTS = 2


def _ffn_stream_body(x_ref, g_ref, wg_hbm, wu_hbm, wd_hbm, g2_ref, *refs, emit_h, units):
    refs = list(refs)
    acc_ref = refs.pop(0)
    n_ref = refs.pop(0) if emit_h else acc_ref
    xn_ref, wg_buf, wu_buf, wd_buf, sem = refs
    i = pl.program_id(0)
    n_units = len(units)

    def slot_of(u):
        return u % W_SLOTS if n_units % W_SLOTS == 0 else (i * n_units + u) % W_SLOTS

    def copy(w_hbm, w_buf, which, u):
        first, count = units[u % n_units]
        slot = slot_of(u)
        return pltpu.make_async_copy(w_hbm.at[pl.ds(first, count)], w_buf.at[slot, pl.ds(0, count)],
                                     sem.at[which, slot])

    def gate_up(u):
        return (copy(wg_hbm, wg_buf, 0, u), copy(wu_hbm, wu_buf, 1, u))

    def down(u):
        return (copy(wd_hbm, wd_buf, 2, u),)

    @pl.when(i == 0)
    def _():
        for cp in gate_up(0) + gate_up(1) + down(0):
            cp.start()
        for cp in gate_up(0):
            cp.wait()

    xn_ref[...] = _rms(x_ref[...], g_ref[...]).astype(BF16)
    for u, (_, count) in enumerate(units):
        slot = slot_of(u)
        xn = xn_ref[...]
        acts = []
        for q in range(count):
            halves = []
            for c in range(0, wg_buf.shape[3], MXU_COLUMNS):
                a = _dot(xn, wg_buf[slot, q, :, c:c + MXU_COLUMNS])
                b = _dot(xn, wu_buf[slot, q, :, c:c + MXU_COLUMNS])
                halves.append(((a * jax.nn.sigmoid(a)) * b).astype(BF16))
            acts.append(jnp.concatenate(halves, axis=1))
        for cp in down(u) + gate_up(u + 1):
            cp.wait()
        for cp in gate_up(u + 2) + down(u + 1):
            cp.start()
        part = _dot(acts[0], wd_buf[slot, 0])
        for q in range(1, count):
            part = part + _dot(acts[q], wd_buf[slot, q])
        if u == 0:
            acc_ref[...] = part
        else:
            acc_ref[...] += part

    h = x_ref[...] + 0.5 * acc_ref[...]
    if emit_h:
        acc_ref[...] = h
    n_ref[...] = _rms(h, g2_ref[...]).astype(n_ref.dtype)

    @pl.when(i == pl.num_programs(0) - 1)
    def _():
        for cp in gate_up(n_units + 1) + down(n_units):
            cp.wait()


def _ffn_stream(x, g, wg, wu, wd, g2, *, tm, unit, emit_h, name):
    rows, d = x.shape
    n_f, _, tf = wg.shape
    n_i = rows // tm
    assert rows == n_i * tm and wu.shape == wg.shape and wd.shape == (n_f, tf, d)
    units = tuple((first, min(unit, n_f - first)) for first in range(0, n_f, unit))
    assert len(units) >= W_SLOTS
    row = lambda i: (i, 0)
    hbm = pl.BlockSpec(memory_space=pl.ANY)
    in_specs = [pl.BlockSpec((tm, d), row), pl.BlockSpec((1, d), lambda i: (0, 0)), hbm, hbm, hbm,
                pl.BlockSpec((1, d), lambda i: (0, 0))]
    out_shape = [jax.ShapeDtypeStruct((rows, d), F32)]
    out_specs = [pl.BlockSpec((tm, d), row)]
    if emit_h:
        out_shape.append(jax.ShapeDtypeStruct((rows, d), BF16))
        out_specs.append(pl.BlockSpec((tm, d), row))
    scratch = [pltpu.VMEM((tm, d), BF16),
               pltpu.VMEM((W_SLOTS, unit, d, tf), BF16), pltpu.VMEM((W_SLOTS, unit, d, tf), BF16),
               pltpu.VMEM((W_SLOTS, unit, tf, d), BF16),
               pltpu.SemaphoreType.DMA((3, W_SLOTS))]
    return pl.pallas_call(
        functools.partial(_ffn_stream_body, emit_h=emit_h, units=units),
        grid=(n_i,),
        in_specs=in_specs,
        out_specs=out_specs,
        out_shape=out_shape,
        scratch_shapes=scratch,
        compiler_params=_params(("arbitrary",)),
        name=name,
    )(x, g, wg, wu, wd, g2)


def _conv3(w_ref, cur, prev1, prev2):
    w = w_ref[...]
    y = w[0:1] * prev2
    y = y + w[1:2] * prev1
    return y + w[2:3] * cur


def _window_sums(e, g):
    p = e + pltpu.roll(e, 1, axis=0)
    for level, shift in ((1, 2), (2, 4), (3, 8)):
        p = jnp.where(g >= level, p + pltpu.roll(p, shift, axis=0), p)
    return p


def _inv_window(g):
    return jnp.where(g == 0, 0.5, jnp.where(g == 1, 0.25, jnp.where(g == 2, 0.125, 0.0625))).astype(F32)


def _mix_prompt_body(u_ref, wb_ref, wc_ref, wv_ref, wz_ref, wgc_ref, wgp_ref, cw_ref, cinit_ref,
                     zinit_ref, bc_ref, pooled_ref, sgc_ref, sgp_ref, cvlast_ref, zlast_ref,
                     cpast_ref, zpast_ref, *, tm, tiles_per_seq, n_j):
    j = pl.program_id(0)
    i = pl.program_id(1)
    t = i % tiles_per_seq
    b = i // tiles_per_seq
    g = j // (n_j // N_GROUPS)
    u = u_ref[...]

    @pl.when(t == 0)
    def _():
        cpast_ref[...] = cinit_ref[...]
        zpast_ref[...] = zinit_ref[...]

    cv = _dot(u, _join_chunks(wc_ref)) * _dot(u, _join_chunks(wv_ref))
    ext = jnp.concatenate([cpast_ref[...], cv], axis=0)
    y = _conv3(cw_ref, cv, pltpu.roll(ext, 1, axis=0)[F32_SUBLANES:], pltpu.roll(ext, 2, axis=0)[F32_SUBLANES:])
    bc_ref[0] = (_dot(u, _join_chunks(wb_ref)) * y).astype(BF16)
    cpast_ref[...] = cv[tm - F32_SUBLANES:]

    z = _dot(u, _join_chunks(wz_ref))
    sums = _window_sums(jnp.concatenate([zpast_ref[...], z], axis=0), g)[MAX_WIN:]
    pooled_ref[0] = (sums * _inv_window(g) - z).astype(BF16)
    zpast_ref[...] = z[tm - MAX_WIN:]

    sgc_ref[0] = jax.nn.sigmoid(_dot(u, _join_chunks(wgc_ref)))
    sgp_ref[0] = jax.nn.sigmoid(_dot(u, _join_chunks(wgp_ref)))

    @pl.when(t == tiles_per_seq - 1)
    def _():
        cvlast_ref[b] = cpast_ref[pl.ds(F32_SUBLANES - (CONV_W - 1), CONV_W - 1), :]
        zlast_ref[b] = zpast_ref[...]


def _mix_prompt(u, w_groups, conv_w, cinit, zinit, *, n_seq, tm, n_j):
    rows, d = u.shape
    d_pool = zinit.shape[1]
    tc, tz = d // n_j, d_pool // n_j
    seq = rows // n_seq
    tiles_per_seq = seq // tm
    n_i = rows // tm
    assert seq == tiles_per_seq * tm and n_j % N_GROUPS == 0 and tm >= MAX_WIN
    blk = lambda w: pl.BlockSpec((1, tm, w), lambda j, i: (j, i, 0))
    wcol = lambda w, a: pl.BlockSpec((w // a.shape[2], d, a.shape[2]), lambda j, i: (j, 0, 0))
    widths = (tc, tc, tc, tz, tc, tc)
    assert all(w % a.shape[2] == 0 for w, a in zip(widths, w_groups))
    in_specs = ([pl.BlockSpec((tm, d), lambda j, i: (i, 0))]
                + [wcol(w, a) for w, a in zip(widths, w_groups)]
                + [pl.BlockSpec((CONV_W, tc), lambda j, i: (0, j)),
                   pl.BlockSpec((F32_SUBLANES, tc), lambda j, i: (0, j)),
                   pl.BlockSpec((MAX_WIN, tz), lambda j, i: (0, j))])
    out_shape = (jax.ShapeDtypeStruct((n_j, rows, tc), BF16),
                 jax.ShapeDtypeStruct((n_j, rows, tz), BF16),
                 jax.ShapeDtypeStruct((n_j, rows, tc), F32),
                 jax.ShapeDtypeStruct((n_j, rows, tc), F32),
                 jax.ShapeDtypeStruct((n_seq, CONV_W - 1, d), F32),
                 jax.ShapeDtypeStruct((n_seq, MAX_WIN, d_pool), F32))
    out_specs = (blk(tc), blk(tz), blk(tc), blk(tc),
                 pl.BlockSpec((n_seq, CONV_W - 1, tc), lambda j, i: (0, 0, j)),
                 pl.BlockSpec((n_seq, MAX_WIN, tz), lambda j, i: (0, 0, j)))
    scratch = [pltpu.VMEM((F32_SUBLANES, tc), F32), pltpu.VMEM((MAX_WIN, tz), F32)]
    return pl.pallas_call(
        functools.partial(_mix_prompt_body, tm=tm, tiles_per_seq=tiles_per_seq, n_j=n_j),
        grid=(n_j, n_i),
        in_specs=in_specs,
        out_specs=out_specs,
        out_shape=out_shape,
        scratch_shapes=scratch,
        compiler_params=_params(("arbitrary", "arbitrary")),
        name="mix_prompt",
    )(u, *w_groups, conv_w, cinit, zinit)


def _mix_sample_body(u_ref, wb_ref, wc_ref, wv_ref, wz_ref, wgc_ref, wgp_ref, cw_ref, cst_ref,
                     zst_ref, bc_ref, pooled_ref, sgc_ref, sgp_ref, cvall_ref, zall_ref,
                     wb_c, wc_c, wv_c, wz_c, wgc_c, wgp_c, *, n_seq, n_j):
    seq = F32_SUBLANES
    ns = n_seq * seq
    j = pl.program_id(0)
    g = j // (n_j // N_GROUPS)
    u = u_ref[...]
    us = u[:ns]
    tc = cvall_ref.shape[1]
    tz = zall_ref.shape[1]

    cv_all = _dot(u, _weight(wc_ref, wc_c)) * _dot(u, _weight(wv_ref, wv_c))
    cvall_ref[...] = cv_all
    cv = cv_all[:ns]
    t = lax.broadcasted_iota(jnp.int32, (ns, 1), 0) % seq
    state = cst_ref[...]
    per_token = lambda s: jnp.broadcast_to(s, (n_seq, seq, tc)).reshape(ns, tc)
    older, newer = per_token(state[:, 0:1, :]), per_token(state[:, 1:2, :])
    prev1 = jnp.where(t == 0, newer, pltpu.roll(cv, 1, axis=0))
    prev2 = jnp.where(t == 0, older, jnp.where(t == 1, newer, pltpu.roll(cv, 2, axis=0)))
    y = _conv3(cw_ref, cv, prev1, prev2)
    bc_ref[0] = (_dot(us, _weight(wb_ref, wb_c)) * y).astype(BF16)

    z_all = _dot(u, _weight(wz_ref, wz_c))
    zall_ref[...] = z_all
    z = z_all[:ns]
    per = MAX_WIN + seq
    ext = jnp.concatenate([zst_ref[...], z.reshape(n_seq, seq, tz)], axis=1)
    sums = _window_sums(ext.reshape(n_seq * per, tz), g)
    sums = sums.reshape(n_seq, per, tz)[:, MAX_WIN:, :].reshape(ns, tz)
    pooled_ref[0] = (sums * _inv_window(g) - z).astype(BF16)

    sgc_ref[0] = jax.nn.sigmoid(_dot(us, _weight(wgc_ref, wgc_c)))
    sgp_ref[0] = jax.nn.sigmoid(_dot(us, _weight(wgp_ref, wgp_c)))


def _mix_sample(u, w_in, conv_w, cst, zst, *, n_j):
    rows, d = u.shape
    n_seq, _, d_pool = zst.shape
    ns = n_seq * F32_SUBLANES
    tc, tz = d // n_j, d_pool // n_j
    assert n_j % N_GROUPS == 0 and cst.shape == (n_seq, CONV_W - 1, d) and CONV_W == 3
    assert (3 * d + d_pool) % tc == 0
    conv_blocks = d // tc
    z0 = (3 * d) // tz
    gc0 = (3 * d + d_pool) // tc
    wcol = lambda w, off: pl.BlockSpec((d, w), lambda j: (0, off + j))
    w_specs = [wcol(tc, 0), wcol(tc, conv_blocks), wcol(tc, 2 * conv_blocks), wcol(tz, z0),
               wcol(tc, gc0), wcol(tc, gc0 + conv_blocks)]
    col = lambda r, w: pl.BlockSpec((r, w), lambda j: (0, j))
    in_specs = ([_resident((rows, d), lambda j: (0, 0))] + w_specs
                + [col(CONV_W, tc),
                   pl.BlockSpec((n_seq, CONV_W - 1, tc), lambda j: (0, 0, j)),
                   pl.BlockSpec((n_seq, MAX_WIN, tz), lambda j: (0, 0, j))])
    group_widths = (d, d, d, d_pool, d, d)
    out_shape = ([jax.ShapeDtypeStruct((n_j, ns, tc), BF16),
                  jax.ShapeDtypeStruct((n_j, ns, tz), BF16),
                  jax.ShapeDtypeStruct((n_j, ns, tc), F32),
                  jax.ShapeDtypeStruct((n_j, ns, tc), F32),
                  jax.ShapeDtypeStruct((rows, d), F32),
                  jax.ShapeDtypeStruct((rows, d_pool), F32)]
                 + [jax.ShapeDtypeStruct((n_j, d, w // n_j), BF16) for w in group_widths])
    chunk = lambda r, w: pl.BlockSpec((1, r, w), lambda j: (j, 0, 0))
    out_specs = ([chunk(ns, tc), chunk(ns, tz), chunk(ns, tc), chunk(ns, tc), col(rows, tc), col(rows, tz)]
                 + [chunk(d, w // n_j) for w in group_widths])
    return pl.pallas_call(
        functools.partial(_mix_sample_body, n_seq=n_seq, n_j=n_j),
        grid=(n_j,),
        in_specs=in_specs,
        out_specs=out_specs,
        out_shape=out_shape,
        compiler_params=_params(("arbitrary",)),
        name="mix_sample",
    )(u, w_in, w_in, w_in, w_in, w_in, w_in, conv_w, cst, zst)


def _join_chunks(ref):
    return jnp.concatenate([ref[c] for c in range(ref.shape[0])], axis=1)


def _merged_block(bc, pooled, sgc, sgp, pscale, wco, wpg):
    if wco.shape[1] <= MXU_COLUMNS:
        half = bc.shape[0] // 2
        y_conv = jnp.concatenate([_dot(bc[:half], wco), _dot(bc[half:], wco)], axis=0)
    else:
        y_conv = _dot(bc, wco)
    y_pool = _dot(pooled, wpg) * pscale
    return (sgc * y_conv + sgp * y_pool).astype(BF16)


def _outproj_body(bc_ref, pooled_ref, sgc_ref, sgp_ref, h_ref, wco_ref, wpg_ref, ps_ref, wo_ref,
                  h2_ref, wco_c, wpg_c, wo_c):
    @pl.when(pl.program_id(1) == 0)
    def _():
        h2_ref[...] = h_ref[...]

    m = _merged_block(_join_chunks(bc_ref), _join_chunks(pooled_ref), sgc_ref[0], sgp_ref[0], ps_ref[...],
                      _weight(wco_ref, wco_c), _weight(wpg_ref, wpg_c))
    h2_ref[...] += _dot(m, _weight(wo_ref, wo_c))


def _outproj(bc, pooled, sgc, sgp, h, wco, wpg, pscale, wo, *, tn, name):
    bc_chunks, rows, bc_width = bc.shape
    d = bc_chunks * bc_width
    n_groups, pool_in, group_out = wpg.shape
    n_n = d // tn
    per_group = group_out // tn
    pool_chunks = pooled.shape[0] // n_groups
    assert d == n_n * tn and group_out == per_group * tn and d == n_groups * group_out
    assert sgc.shape == sgp.shape == (n_n, rows, tn) and pool_chunks * pooled.shape[2] == pool_in
    chunk_spec = lambda r: pl.BlockSpec((1, r, tn), lambda i, n: (n // per_group, 0, n % per_group))
    w_specs = [pl.BlockSpec((d, tn), lambda i, n: (0, n)), chunk_spec(pool_in),
               pl.BlockSpec((tn, d), lambda i, n: (n, 0))]
    w_shapes = [jax.ShapeDtypeStruct((n_n, d, tn), BF16),
                jax.ShapeDtypeStruct(wpg.shape, BF16), jax.ShapeDtypeStruct(wo.shape, BF16)]
    w_out_specs = [pl.BlockSpec((1, d, tn), lambda i, n: (n, 0, 0)), w_specs[1], w_specs[2]]
    in_specs = [
        _resident((bc_chunks, rows, bc_width), lambda i, n: (0, 0, 0)),
        pl.BlockSpec((pool_chunks, rows, pooled.shape[2]), lambda i, n: (n // per_group, 0, 0)),
        pl.BlockSpec((1, rows, tn), lambda i, n: (n, 0, 0)),
        pl.BlockSpec((1, rows, tn), lambda i, n: (n, 0, 0)),
        _resident((rows, d), lambda i, n: (0, 0)),
        w_specs[0], w_specs[1],
        pl.BlockSpec((1, tn), lambda i, n: (0, n)),
        w_specs[2],
    ]
    return pl.pallas_call(
        _outproj_body,
        grid=(1, n_n),
        in_specs=in_specs,
        out_specs=[_resident((rows, d), lambda i, n: (0, 0))] + w_out_specs,
        out_shape=[jax.ShapeDtypeStruct((rows, d), F32)] + w_shapes,
        compiler_params=_params(("arbitrary", "arbitrary")),
        name=name,
    )(bc, pooled, sgc, sgp, h, wco, wpg, pscale, wo)


def _outproj_resident_body(bc_ref, pooled_ref, sgc_ref, sgp_ref, h_ref, wco_ref, wpg_ref, ps_ref, wo_ref,
                           h2_ref, *, groups):
    n = pl.program_id(1)
    tn = sgc_ref.shape[2]

    @pl.when(n == 0)
    def _():
        h2_ref[...] = h_ref[...]

    bc = _join_chunks(bc_ref)
    total = None
    for q in range(groups):
        k = n * groups + q
        per = wco_ref.shape[0] // wpg_ref.shape[0]
        wco = jnp.concatenate([wco_ref[k * per + c] for c in range(per)], axis=1)
        m = _merged_block(bc, pooled_ref[q], sgc_ref[q], sgp_ref[q], ps_ref[:, q * tn:(q + 1) * tn],
                          wco, wpg_ref[k])
        part = _dot(m, wo_ref[k])
        total = part if total is None else total + part
    h2_ref[...] += total


def _outproj_resident(bc, pooled, sgc, sgp, h, wco, wpg, pscale, wo, *, tm, groups, name):
    n_groups, rows, tn = sgc.shape
    d = n_groups * tn
    n_i, n_n = rows // tm, n_groups // groups
    assert rows == n_i * tm and n_groups == n_n * groups
    assert bc.shape == sgp.shape == (n_groups, rows, tn) and pooled.shape[:2] == (n_groups, rows)
    assert wco.shape[0] % n_groups == 0 and wco.shape[0] * wco.shape[2] == d and wco.shape[1] == d
    assert wo.shape == (n_groups, tn, d) and wpg.shape[0] == n_groups
    whole = lambda a: _resident(a.shape, lambda i, n: (0, 0, 0))
    per_step = lambda w: pl.BlockSpec((groups, tm, w), lambda i, n: (n, i, 0))
    in_specs = [
        pl.BlockSpec((n_groups, tm, tn), lambda i, n: (0, i, 0)),
        per_step(pooled.shape[2]), per_step(tn), per_step(tn),
        pl.BlockSpec((tm, d), lambda i, n: (i, 0)),
        whole(wco), whole(wpg),
        pl.BlockSpec((1, groups * tn), lambda i, n: (0, n)),
        whole(wo),
    ]
    return pl.pallas_call(
        functools.partial(_outproj_resident_body, groups=groups),
        grid=(n_i, n_n),
        in_specs=in_specs,
        out_specs=pl.BlockSpec((tm, d), lambda i, n: (i, 0)),
        out_shape=jax.ShapeDtypeStruct((rows, d), F32),
        compiler_params=_params(("arbitrary", "arbitrary")),
        name=name,
    )(bc, pooled, sgc, sgp, h, wco, wpg, pscale, wo)


def kernel(x_prompt, x_sample, state_conv, state_pool, meta_tokens, norm_ffn1, w_ffn1_gate,
           w_ffn1_up, w_ffn1_down, norm_mix, w_in, conv_w, w_conv_out, w_pool_group, pool_scale,
           w_o, norm_ffn2, w_ffn2_gate, w_ffn2_up, w_ffn2_down, norm_final):
    batch, seq, d = x_prompt.shape
    dec_batch, dec_seq, _ = x_sample.shape
    depth = w_in.shape[0]
    d_pool = state_pool.shape[-1]
    assert depth == 1 and dec_seq == F32_SUBLANES and meta_tokens.shape[0] == N_META == MAX_WIN
    assert state_conv.shape[2] == CONV_W - 1 and state_pool.shape[2] == MAX_WIN - 1
    n_s = dec_batch * dec_seq

    row = lambda v: v.reshape(1, d)
    g1, gm, g2, gf = row(norm_ffn1[0]), row(norm_mix[0]), row(norm_ffn2[0]), row(norm_final)
    pscale = row(pool_scale[0])
    cw = conv_w[0]

    h_s, u_s, wg1, wu1, wd1 = _ffn_cast((x_sample, meta_tokens.astype(F32)), g1,
                                        w_ffn1_gate[0], w_ffn1_up[0], w_ffn1_down[0], gm,
                                        tf=256, w_chunk=256, emit_h=True, name="ffn1_sample")
    zst = jnp.pad(state_pool[0], ((0, 0), (1, 0), (0, 0)))
    bc_s, pooled_s, sgc_s, sgp_s, cv_all, z_all, *w_groups = _mix_sample(u_s, w_in[0], cw, state_conv[0], zst, n_j=8)
    h2_s, wco, wpg, wo = _outproj(bc_s, pooled_s, sgc_s, sgp_s, h_s, w_conv_out[0], w_pool_group[0],
                                  pscale, w_o[0], tn=256, name="outproj_sample")
    y_s, wg2, wu2, wd2 = _ffn_cast((h2_s,), g2, w_ffn2_gate[0], w_ffn2_up[0], w_ffn2_down[0], gf,
                                   tf=256, w_chunk=256, emit_h=False, name="ffn2_sample")

    cv_s = cv_all[:n_s].reshape(dec_batch, dec_seq, d)
    new_conv_sample = cv_s[:, dec_seq - (CONV_W - 1):]
    new_pool_sample = jnp.concatenate(
        [state_pool[0][:, dec_seq:], z_all[:n_s].reshape(dec_batch, dec_seq, d_pool)], axis=1)

    cinit = jnp.pad(cv_all[n_s + N_META - (CONV_W - 1):], ((F32_SUBLANES - (CONV_W - 1), 0), (0, 0)))
    zinit = z_all[n_s:]
    xp = x_prompt.reshape(batch * seq, d)
    down_chunks = lambda wd, wg: wd.reshape(wg.shape[0], wg.shape[2], d)
    h_p, u_p = _ffn_stream(xp, g1, wg1, wu1, down_chunks(wd1, wg1), gm, tm=512, unit=4, emit_h=True,
                           name="ffn1_prompt")
    bc_p, pooled_p, sgc_p, sgp_p, cvlast, zlast = _mix_prompt(u_p, w_groups, cw, cinit, zinit,
                                                              n_seq=batch, tm=1024, n_j=4)
    tn = wpg.shape[2]
    h2_p = _outproj_resident(bc_p, pooled_p, sgc_p, sgp_p, h_p, wco, wpg, pscale, wo.reshape(d // tn, tn, d),
                             tm=512, groups=2, name="outproj_prompt")
    (y_p,) = _ffn_stream(h2_p, g2, wg2, wu2, down_chunks(wd2, wg2), gf, tm=512, unit=4, emit_h=False,
                         name="ffn2_prompt")

    return (y_p.reshape(batch, seq, d),
            y_s.reshape(dec_batch, dec_seq, d),
            cvlast[None],
            zlast[:, 1:][None],
            new_conv_sample[None],
            new_pool_sample[None])
```

```python
import functools

import jax
import jax.numpy as jnp
from jax import lax
from jax.experimental import pallas as pl
from jax.experimental.pallas import tpu as pltpu

F32 = jnp.float32
BF16 = jnp.bfloat16

EPS = 1e-6
N_META = 16
CONV_W = 3
POOL_WINDOWS = (2, 4, 8, 16)
MAX_WIN = max(POOL_WINDOWS)
N_GROUPS = len(POOL_WINDOWS)

F32_SUBLANES = 8
MXU_COLUMNS = 256
ROW_PIECES = 8
V7X_VMEM_LIMIT_BYTES = 60 * 1024 * 1024


def _params(semantics):
    return pltpu.CompilerParams(dimension_semantics=semantics,
                                vmem_limit_bytes=V7X_VMEM_LIMIT_BYTES)


def _dot(a, b):
    return jnp.dot(a, b, preferred_element_type=F32)


def _rms(x, g):
    ms = jnp.mean(x * x, axis=-1, keepdims=True)
    return (x * lax.rsqrt(ms + EPS)) * g


def _resident(shape, index_map):
    return pl.BlockSpec(shape, index_map, pipeline_mode=pl.Buffered(1))


def _block(ref):
    return ref[0] if len(ref.shape) == 3 else ref[...]


def _set_block(ref, value):
    if len(ref.shape) == 3:
        ref[0] = value
    else:
        ref[...] = value


def _weight(w_ref, copy_ref):
    w = _block(w_ref).astype(BF16)
    if copy_ref is not None:
        _set_block(copy_ref, w)
    return w


def _ffn_cast_body(*refs, n_parts, emit_h):
    refs = list(refs)
    x_refs = [refs.pop(0) for _ in range(n_parts)]
    g_ref, wg_ref, wu_ref, wd_ref, g2_ref, acc_ref = (refs.pop(0) for _ in range(6))
    n_ref = refs.pop(0) if emit_h else acc_ref
    wg_c, wu_c, wd_c, xn_ref = refs
    f = pl.program_id(0)

    def x():
        return jnp.concatenate([r[...].reshape(-1, r.shape[-1]) for r in x_refs], axis=0)

    @pl.when(f == 0)
    def _():
        xn_ref[...] = _rms(x(), g_ref[...]).astype(BF16)
        acc_ref[...] = jnp.zeros(acc_ref.shape, F32)

    xn = xn_ref[...]
    a = _dot(xn, _weight(wg_ref, wg_c))
    b = _dot(xn, _weight(wu_ref, wu_c))
    act = ((a * jax.nn.sigmoid(a)) * b).astype(BF16)
    acc_ref[...] += _dot(act, _weight(wd_ref, wd_c))

    @pl.when(f == pl.num_programs(0) - 1)
    def _():
        h = x() + 0.5 * acc_ref[...]
        if emit_h:
            acc_ref[...] = h
        n_ref[...] = _rms(h, g2_ref[...]).astype(n_ref.dtype)


def _ffn_cast(x_parts, g, wg, wu, wd, g2, *, tf, w_chunk, emit_h, name):
    d, d_ff = wg.shape
    part_rows = [p.size // d for p in x_parts]
    rows = sum(part_rows)
    n_f = d_ff // tf
    assert d_ff == n_f * tf and w_chunk % tf == 0 and all(r % F32_SUBLANES == 0 for r in part_rows)
    assert all(p.ndim == 2 or p.shape[1] == F32_SUBLANES for p in x_parts)
    per_chunk = w_chunk // tf
    whole = lambda shape: _resident(shape, lambda f: (0,) * len(shape))
    col_spec = pl.BlockSpec((d, tf), lambda f: (0, f))
    wd_spec = pl.BlockSpec((tf, d), lambda f: (f, 0))
    chunk_spec = pl.BlockSpec((1, d, tf), lambda f: (f // per_chunk, 0, f % per_chunk))
    gain_spec = pl.BlockSpec((1, d), lambda f: (0, 0))
    in_specs = [whole(p.shape) for p in x_parts] + [gain_spec, col_spec, col_spec, wd_spec, gain_spec]
    out_shape = [jax.ShapeDtypeStruct((rows, d), F32)]
    out_specs = [whole((rows, d))]
    if emit_h:
        out_shape.append(jax.ShapeDtypeStruct((rows, d), BF16))
        out_specs.append(whole((rows, d)))
    chunked = jax.ShapeDtypeStruct((d_ff // w_chunk, d, w_chunk), BF16)
    return pl.pallas_call(
        functools.partial(_ffn_cast_body, n_parts=len(x_parts), emit_h=emit_h),
        grid=(n_f,),
        in_specs=in_specs,
        out_specs=out_specs + [chunk_spec, chunk_spec, wd_spec],
        out_shape=out_shape + [chunked, chunked, jax.ShapeDtypeStruct(wd.shape, BF16)],
        scratch_shapes=[pltpu.VMEM((rows, d), BF16)],
        compiler_params=_params(("arbitrary",)),
        name=name,
    )(*x_parts, g, wg, wu, wd, g2)


W_SLOTS = 2


def _ffn_stream_body(x_ref, g_ref, wg_hbm, wu_hbm, wd_hbm, g2_ref, *refs, emit_h, units):
    refs = list(refs)
    acc_ref = refs.pop(0)
    n_ref = refs.pop(0) if emit_h else acc_ref
    xn_ref, wg_buf, wu_buf, wd_buf, sem = refs
    i = pl.program_id(0)
    n_units = len(units)

    def slot_of(u):
        return u % W_SLOTS if n_units % W_SLOTS == 0 else (i * n_units + u) % W_SLOTS

    def copy(w_hbm, w_buf, which, u):
        first, count = units[u % n_units]
        slot = slot_of(u)
        return pltpu.make_async_copy(w_hbm.at[pl.ds(first, count)], w_buf.at[slot, pl.ds(0, count)],
                                     sem.at[which, slot])

    def gate_up(u):
        return (copy(wg_hbm, wg_buf, 0, u), copy(wu_hbm, wu_buf, 1, u))

    def down(u):
        return (copy(wd_hbm, wd_buf, 2, u),)

    @pl.when(i == 0)
    def _():
        for cp in gate_up(0) + gate_up(1) + down(0):
            cp.start()
        for cp in gate_up(0):
            cp.wait()

    xn_ref[...] = _rms(x_ref[...], g_ref[...]).astype(BF16)
    for u, (_, count) in enumerate(units):
        slot = slot_of(u)
        xn = xn_ref[...]
        acts = []
        for q in range(count):
            halves = []
            for c in range(0, wg_buf.shape[3], MXU_COLUMNS):
                a = _dot(xn, wg_buf[slot, q, :, c:c + MXU_COLUMNS])
                b = _dot(xn, wu_buf[slot, q, :, c:c + MXU_COLUMNS])
                halves.append(((a * jax.nn.sigmoid(a)) * b).astype(BF16))
            acts.append(jnp.concatenate(halves, axis=1))
        for cp in down(u) + gate_up(u + 1):
            cp.wait()
        for cp in gate_up(u + 2) + down(u + 1):
            cp.start()
        part = _dot(acts[0], wd_buf[slot, 0])
        for q in range(1, count):
            part = part + _dot(acts[q], wd_buf[slot, q])
        if u == 0:
            acc_ref[...] = part
        else:
            acc_ref[...] += part

    h = x_ref[...] + 0.5 * acc_ref[...]
    if emit_h:
        acc_ref[...] = h
    n_ref[...] = _rms(h, g2_ref[...]).astype(n_ref.dtype)

    @pl.when(i == pl.num_programs(0) - 1)
    def _():
        for cp in gate_up(n_units + 1) + down(n_units):
            cp.wait()


def _ffn_stream(x, g, wg, wu, wd, g2, *, tm, unit, emit_h, name):
    rows, d = x.shape
    n_f, _, tf = wg.shape
    n_i = rows // tm
    assert rows == n_i * tm and wu.shape == wg.shape and wd.shape == (n_f, tf, d)
    units = tuple((first, min(unit, n_f - first)) for first in range(0, n_f, unit))
    assert len(units) >= W_SLOTS
    row = lambda i: (i, 0)
    hbm = pl.BlockSpec(memory_space=pl.ANY)
    in_specs = [pl.BlockSpec((tm, d), row), pl.BlockSpec((1, d), lambda i: (0, 0)), hbm, hbm, hbm,
                pl.BlockSpec((1, d), lambda i: (0, 0))]
    out_shape = [jax.ShapeDtypeStruct((rows, d), F32)]
    out_specs = [pl.BlockSpec((tm, d), row)]
    if emit_h:
        out_shape.append(jax.ShapeDtypeStruct((rows, d), BF16))
        out_specs.append(pl.BlockSpec((tm, d), row))
    scratch = [pltpu.VMEM((tm, d), BF16),
               pltpu.VMEM((W_SLOTS, unit, d, tf), BF16), pltpu.VMEM((W_SLOTS, unit, d, tf), BF16),
               pltpu.VMEM((W_SLOTS, unit, tf, d), BF16),
               pltpu.SemaphoreType.DMA((3, W_SLOTS))]
    return pl.pallas_call(
        functools.partial(_ffn_stream_body, emit_h=emit_h, units=units),
        grid=(n_i,),
        in_specs=in_specs,
        out_specs=out_specs,
        out_shape=out_shape,
        scratch_shapes=scratch,
        compiler_params=_params(("arbitrary",)),
        name=name,
    )(x, g, wg, wu, wd, g2)


def _conv3(w_ref, cur, prev1, prev2):
    w = w_ref[...]
    y = w[0:1] * prev2
    y = y + w[1:2] * prev1
    return y + w[2:3] * cur


def _window_sums(e, g):
    p = e + pltpu.roll(e, 1, axis=0)
    for level, shift in ((1, 2), (2, 4), (3, 8)):
        p = jnp.where(g >= level, p + pltpu.roll(p, shift, axis=0), p)
    return p


def _inv_window(g):
    return jnp.where(g == 0, 0.5, jnp.where(g == 1, 0.25, jnp.where(g == 2, 0.125, 0.0625))).astype(F32)


def _mix_prompt_body(u_ref, wb_ref, wc_ref, wv_ref, wz_ref, wgc_ref, wgp_ref, cw_ref, cinit_ref,
                     zinit_ref, bc_ref, pooled_ref, sgc_ref, sgp_ref, cvlast_ref, zlast_ref,
                     cpast_ref, zpast_ref, *, tm, tiles_per_seq, n_j):
    j = pl.program_id(0)
    i = pl.program_id(1)
    t = i % tiles_per_seq
    b = i // tiles_per_seq
    g = j // (n_j // N_GROUPS)

    @pl.when(t == 0)
    def _():
        cpast_ref[...] = cinit_ref[...]
        zpast_ref[...] = zinit_ref[...]

    piece = tm // ROW_PIECES
    for p in range(ROW_PIECES):
        rows = pl.ds(p * piece, piece)
        u = u_ref[rows, :]

        cv = _dot(u, wc_ref[...]) * _dot(u, wv_ref[...])
        ext = jnp.concatenate([cpast_ref[...], cv], axis=0)
        y = _conv3(cw_ref, cv, pltpu.roll(ext, 1, axis=0)[F32_SUBLANES:], pltpu.roll(ext, 2, axis=0)[F32_SUBLANES:])
        bc_ref[0, rows, :] = (_dot(u, wb_ref[...]) * y).astype(BF16)
        cpast_ref[...] = cv[piece - F32_SUBLANES:]

        z = _dot(u, wz_ref[...])
        sums = _window_sums(jnp.concatenate([zpast_ref[...], z], axis=0), g)[MAX_WIN:]
        pooled_ref[0, rows, :] = (sums * _inv_window(g) - z).astype(BF16)
        zpast_ref[...] = z[piece - MAX_WIN:]

        sgc_ref[0, rows, :] = jax.nn.sigmoid(_dot(u, wgc_ref[...]))
        sgp_ref[0, rows, :] = jax.nn.sigmoid(_dot(u, wgp_ref[...]))

    @pl.when(t == tiles_per_seq - 1)
    def _():
        cvlast_ref[b] = cpast_ref[pl.ds(F32_SUBLANES - (CONV_W - 1), CONV_W - 1), :]
        zlast_ref[b] = zpast_ref[...]


def _mix_prompt(u, w_groups, conv_w, cinit, zinit, *, n_seq, tm, n_j):
    rows, d = u.shape
    d_pool = zinit.shape[1]
    tc, tz = d // n_j, d_pool // n_j
    seq = rows // n_seq
    tiles_per_seq = seq // tm
    n_i = rows // tm
    assert seq == tiles_per_seq * tm and n_j % N_GROUPS == 0 and tm >= MAX_WIN
    blk = lambda w: pl.BlockSpec((1, tm, w), lambda j, i: (j, i, 0))
    wcol = lambda w: pl.BlockSpec((d, w), lambda j, i: (0, j))
    in_specs = ([pl.BlockSpec((tm, d), lambda j, i: (i, 0))]
                + [wcol(tc), wcol(tc), wcol(tc), wcol(tz), wcol(tc), wcol(tc)]
                + [pl.BlockSpec((CONV_W, tc), lambda j, i: (0, j)),
                   pl.BlockSpec((F32_SUBLANES, tc), lambda j, i: (0, j)),
                   pl.BlockSpec((MAX_WIN, tz), lambda j, i: (0, j))])
    out_shape = (jax.ShapeDtypeStruct((n_j, rows, tc), BF16),
                 jax.ShapeDtypeStruct((n_j, rows, tz), BF16),
                 jax.ShapeDtypeStruct((n_j, rows, tc), F32),
                 jax.ShapeDtypeStruct((n_j, rows, tc), F32),
                 jax.ShapeDtypeStruct((n_seq, CONV_W - 1, d), F32),
                 jax.ShapeDtypeStruct((n_seq, MAX_WIN, d_pool), F32))
    out_specs = (blk(tc), blk(tz), blk(tc), blk(tc),
                 pl.BlockSpec((n_seq, CONV_W - 1, tc), lambda j, i: (0, 0, j)),
                 pl.BlockSpec((n_seq, MAX_WIN, tz), lambda j, i: (0, 0, j)))
    scratch = [pltpu.VMEM((F32_SUBLANES, tc), F32), pltpu.VMEM((MAX_WIN, tz), F32)]
    return pl.pallas_call(
        functools.partial(_mix_prompt_body, tm=tm, tiles_per_seq=tiles_per_seq, n_j=n_j),
        grid=(n_j, n_i),
        in_specs=in_specs,
        out_specs=out_specs,
        out_shape=out_shape,
        scratch_shapes=scratch,
        compiler_params=_params(("arbitrary", "arbitrary")),
        name="mix_prompt",
    )(u, *w_groups, conv_w, cinit, zinit)


def _mix_sample_body(u_ref, wb_ref, wc_ref, wv_ref, wz_ref, wgc_ref, wgp_ref, cw_ref, cst_ref,
                     zst_ref, bc_ref, pooled_ref, sgc_ref, sgp_ref, cvall_ref, zall_ref,
                     wb_c, wc_c, wv_c, wz_c, wgc_c, wgp_c, *, n_seq, n_j):
    seq = F32_SUBLANES
    ns = n_seq * seq
    j = pl.program_id(0)
    g = j // (n_j // N_GROUPS)
    u = u_ref[...]
    us = u[:ns]
    tc = cvall_ref.shape[1]
    tz = zall_ref.shape[1]

    cv_all = _dot(u, _weight(wc_ref, wc_c)) * _dot(u, _weight(wv_ref, wv_c))
    cvall_ref[...] = cv_all
    cv = cv_all[:ns]
    t = lax.broadcasted_iota(jnp.int32, (ns, 1), 0) % seq
    state = cst_ref[...]
    per_token = lambda s: jnp.broadcast_to(s, (n_seq, seq, tc)).reshape(ns, tc)
    older, newer = per_token(state[:, 0:1, :]), per_token(state[:, 1:2, :])
    prev1 = jnp.where(t == 0, newer, pltpu.roll(cv, 1, axis=0))
    prev2 = jnp.where(t == 0, older, jnp.where(t == 1, newer, pltpu.roll(cv, 2, axis=0)))
    y = _conv3(cw_ref, cv, prev1, prev2)
    bc_ref[0] = (_dot(us, _weight(wb_ref, wb_c)) * y).astype(BF16)

    z_all = _dot(u, _weight(wz_ref, wz_c))
    zall_ref[...] = z_all
    z = z_all[:ns]
    per = MAX_WIN + seq
    ext = jnp.concatenate([zst_ref[...], z.reshape(n_seq, seq, tz)], axis=1)
    sums = _window_sums(ext.reshape(n_seq * per, tz), g)
    sums = sums.reshape(n_seq, per, tz)[:, MAX_WIN:, :].reshape(ns, tz)
    pooled_ref[0] = (sums * _inv_window(g) - z).astype(BF16)

    sgc_ref[0] = jax.nn.sigmoid(_dot(us, _weight(wgc_ref, wgc_c)))
    sgp_ref[0] = jax.nn.sigmoid(_dot(us, _weight(wgp_ref, wgp_c)))


def _mix_sample(u, w_in, conv_w, cst, zst, *, n_j):
    rows, d = u.shape
    n_seq, _, d_pool = zst.shape
    ns = n_seq * F32_SUBLANES
    tc, tz = d // n_j, d_pool // n_j
    assert n_j % N_GROUPS == 0 and cst.shape == (n_seq, CONV_W - 1, d) and CONV_W == 3
    assert (3 * d + d_pool) % tc == 0
    conv_blocks = d // tc
    z0 = (3 * d) // tz
    gc0 = (3 * d + d_pool) // tc
    wcol = lambda w, off: pl.BlockSpec((d, w), lambda j: (0, off + j))
    w_specs = [wcol(tc, 0), wcol(tc, conv_blocks), wcol(tc, 2 * conv_blocks), wcol(tz, z0),
               wcol(tc, gc0), wcol(tc, gc0 + conv_blocks)]
    col = lambda r, w: pl.BlockSpec((r, w), lambda j: (0, j))
    in_specs = ([_resident((rows, d), lambda j: (0, 0))] + w_specs
                + [col(CONV_W, tc),
                   pl.BlockSpec((n_seq, CONV_W - 1, tc), lambda j: (0, 0, j)),
                   pl.BlockSpec((n_seq, MAX_WIN, tz), lambda j: (0, 0, j))])
    group_widths = (d, d, d, d_pool, d, d)
    out_shape = ([jax.ShapeDtypeStruct((n_j, ns, tc), BF16),
                  jax.ShapeDtypeStruct((n_j, ns, tz), BF16),
                  jax.ShapeDtypeStruct((n_j, ns, tc), F32),
                  jax.ShapeDtypeStruct((n_j, ns, tc), F32),
                  jax.ShapeDtypeStruct((rows, d), F32),
                  jax.ShapeDtypeStruct((rows, d_pool), F32)]
                 + [jax.ShapeDtypeStruct((d, w), BF16) for w in group_widths])
    chunk = lambda w: pl.BlockSpec((1, ns, w), lambda j: (j, 0, 0))
    out_specs = ([chunk(tc), chunk(tz), chunk(tc), chunk(tc), col(rows, tc), col(rows, tz)]
                 + [col(d, tz if w == d_pool else tc) for w in group_widths])
    return pl.pallas_call(
        functools.partial(_mix_sample_body, n_seq=n_seq, n_j=n_j),
        grid=(n_j,),
        in_specs=in_specs,
        out_specs=out_specs,
        out_shape=out_shape,
        compiler_params=_params(("arbitrary",)),
        name="mix_sample",
    )(u, w_in, w_in, w_in, w_in, w_in, w_in, conv_w, cst, zst)


def _join_chunks(ref):
    return jnp.concatenate([ref[c] for c in range(ref.shape[0])], axis=1)


def _merged_block(bc, pooled, sgc, sgp, pscale, wco, wpg):
    if wco.shape[1] <= MXU_COLUMNS:
        half = bc.shape[0] // 2
        y_conv = jnp.concatenate([_dot(bc[:half], wco), _dot(bc[half:], wco)], axis=0)
    else:
        y_conv = _dot(bc, wco)
    y_pool = _dot(pooled, wpg) * pscale
    return (sgc * y_conv + sgp * y_pool).astype(BF16)


def _outproj_body(bc_ref, pooled_ref, sgc_ref, sgp_ref, h_ref, wco_ref, wpg_ref, ps_ref, wo_ref,
                  h2_ref, wco_c, wpg_c, wo_c):
    @pl.when(pl.program_id(1) == 0)
    def _():
        h2_ref[...] = h_ref[...]

    m = _merged_block(_join_chunks(bc_ref), _join_chunks(pooled_ref), sgc_ref[0], sgp_ref[0], ps_ref[...],
                      _weight(wco_ref, wco_c), _weight(wpg_ref, wpg_c))
    h2_ref[...] += _dot(m, _weight(wo_ref, wo_c))


def _outproj(bc, pooled, sgc, sgp, h, wco, wpg, pscale, wo, *, tn, name):
    bc_chunks, rows, bc_width = bc.shape
    d = bc_chunks * bc_width
    n_groups, pool_in, group_out = wpg.shape
    n_n = d // tn
    per_group = group_out // tn
    pool_chunks = pooled.shape[0] // n_groups
    assert d == n_n * tn and group_out == per_group * tn and d == n_groups * group_out
    assert sgc.shape == sgp.shape == (n_n, rows, tn) and pool_chunks * pooled.shape[2] == pool_in
    chunk_spec = lambda r: pl.BlockSpec((1, r, tn), lambda i, n: (n // per_group, 0, n % per_group))
    w_specs = [pl.BlockSpec((d, tn), lambda i, n: (0, n)), chunk_spec(pool_in),
               pl.BlockSpec((tn, d), lambda i, n: (n, 0))]
    w_shapes = [jax.ShapeDtypeStruct((n_groups, d, group_out), BF16),
                jax.ShapeDtypeStruct(wpg.shape, BF16), jax.ShapeDtypeStruct(wo.shape, BF16)]
    w_out_specs = [chunk_spec(d), w_specs[1], w_specs[2]]
    in_specs = [
        _resident((bc_chunks, rows, bc_width), lambda i, n: (0, 0, 0)),
        pl.BlockSpec((pool_chunks, rows, pooled.shape[2]), lambda i, n: (n // per_group, 0, 0)),
        pl.BlockSpec((1, rows, tn), lambda i, n: (n, 0, 0)),
        pl.BlockSpec((1, rows, tn), lambda i, n: (n, 0, 0)),
        _resident((rows, d), lambda i, n: (0, 0)),
        w_specs[0], w_specs[1],
        pl.BlockSpec((1, tn), lambda i, n: (0, n)),
        w_specs[2],
    ]
    return pl.pallas_call(
        _outproj_body,
        grid=(1, n_n),
        in_specs=in_specs,
        out_specs=[_resident((rows, d), lambda i, n: (0, 0))] + w_out_specs,
        out_shape=[jax.ShapeDtypeStruct((rows, d), F32)] + w_shapes,
        compiler_params=_params(("arbitrary", "arbitrary")),
        name=name,
    )(bc, pooled, sgc, sgp, h, wco, wpg, pscale, wo)


def _outproj_resident_body(bc_ref, pooled_ref, sgc_ref, sgp_ref, h_ref, wco_ref, wpg_ref, ps_ref, wo_ref,
                           h2_ref, *, groups):
    n = pl.program_id(1)
    tn = sgc_ref.shape[2]

    @pl.when(n == 0)
    def _():
        h2_ref[...] = h_ref[...]

    bc = _join_chunks(bc_ref)
    total = None
    for q in range(groups):
        k = n * groups + q
        m = _merged_block(bc, pooled_ref[q], sgc_ref[q], sgp_ref[q], ps_ref[:, q * tn:(q + 1) * tn],
                          wco_ref[k], wpg_ref[k])
        part = _dot(m, wo_ref[k])
        total = part if total is None else total + part
    h2_ref[...] += total


def _outproj_resident(bc, pooled, sgc, sgp, h, wco, wpg, pscale, wo, *, tm, groups, name):
    n_groups, rows, tn = sgc.shape
    d = n_groups * tn
    n_i, n_n = rows // tm, n_groups // groups
    assert rows == n_i * tm and n_groups == n_n * groups
    assert bc.shape == sgp.shape == (n_groups, rows, tn) and pooled.shape[:2] == (n_groups, rows)
    assert wco.shape == (n_groups, d, tn) and wo.shape == (n_groups, tn, d) and wpg.shape[0] == n_groups
    whole = lambda a: _resident(a.shape, lambda i, n: (0, 0, 0))
    per_step = lambda w: pl.BlockSpec((groups, tm, w), lambda i, n: (n, i, 0))
    in_specs = [
        pl.BlockSpec((n_groups, tm, tn), lambda i, n: (0, i, 0)),
        per_step(pooled.shape[2]), per_step(tn), per_step(tn),
        pl.BlockSpec((tm, d), lambda i, n: (i, 0)),
        whole(wco), whole(wpg),
        pl.BlockSpec((1, groups * tn), lambda i, n: (0, n)),
        whole(wo),
    ]
    return pl.pallas_call(
        functools.partial(_outproj_resident_body, groups=groups),
        grid=(n_i, n_n),
        in_specs=in_specs,
        out_specs=pl.BlockSpec((tm, d), lambda i, n: (i, 0)),
        out_shape=jax.ShapeDtypeStruct((rows, d), F32),
        compiler_params=_params(("arbitrary", "arbitrary")),
        name=name,
    )(bc, pooled, sgc, sgp, h, wco, wpg, pscale, wo)


def kernel(x_prompt, x_sample, state_conv, state_pool, meta_tokens, norm_ffn1, w_ffn1_gate,
           w_ffn1_up, w_ffn1_down, norm_mix, w_in, conv_w, w_conv_out, w_pool_group, pool_scale,
           w_o, norm_ffn2, w_ffn2_gate, w_ffn2_up, w_ffn2_down, norm_final):
    batch, seq, d = x_prompt.shape
    dec_batch, dec_seq, _ = x_sample.shape
    depth = w_in.shape[0]
    d_pool = state_pool.shape[-1]
    assert depth == 1 and dec_seq == F32_SUBLANES and meta_tokens.shape[0] == N_META == MAX_WIN
    assert state_conv.shape[2] == CONV_W - 1 and state_pool.shape[2] == MAX_WIN - 1
    n_s = dec_batch * dec_seq

    row = lambda v: v.reshape(1, d)
    g1, gm, g2, gf = row(norm_ffn1[0]), row(norm_mix[0]), row(norm_ffn2[0]), row(norm_final)
    pscale = row(pool_scale[0])
    cw = conv_w[0]

    h_s, u_s, wg1, wu1, wd1 = _ffn_cast((x_sample, meta_tokens.astype(F32)), g1,
                                        w_ffn1_gate[0], w_ffn1_up[0], w_ffn1_down[0], gm,
                                        tf=256, w_chunk=512, emit_h=True, name="ffn1_sample")
    zst = jnp.pad(state_pool[0], ((0, 0), (1, 0), (0, 0)))
    bc_s, pooled_s, sgc_s, sgp_s, cv_all, z_all, *w_groups = _mix_sample(u_s, w_in[0], cw, state_conv[0], zst, n_j=8)
    h2_s, wco, wpg, wo = _outproj(bc_s, pooled_s, sgc_s, sgp_s, h_s, w_conv_out[0], w_pool_group[0],
                                  pscale, w_o[0], tn=256, name="outproj_sample")
    y_s, wg2, wu2, wd2 = _ffn_cast((h2_s,), g2, w_ffn2_gate[0], w_ffn2_up[0], w_ffn2_down[0], gf,
                                   tf=256, w_chunk=512, emit_h=False, name="ffn2_sample")

    cv_s = cv_all[:n_s].reshape(dec_batch, dec_seq, d)
    new_conv_sample = cv_s[:, dec_seq - (CONV_W - 1):]
    new_pool_sample = jnp.concatenate(
        [state_pool[0][:, dec_seq:], z_all[:n_s].reshape(dec_batch, dec_seq, d_pool)], axis=1)

    cinit = jnp.pad(cv_all[n_s + N_META - (CONV_W - 1):], ((F32_SUBLANES - (CONV_W - 1), 0), (0, 0)))
    zinit = z_all[n_s:]
    xp = x_prompt.reshape(batch * seq, d)
    down_chunks = lambda wd, wg: wd.reshape(wg.shape[0], wg.shape[2], d)
    h_p, u_p = _ffn_stream(xp, g1, wg1, wu1, down_chunks(wd1, wg1), gm, tm=512, unit=2, emit_h=True,
                           name="ffn1_prompt")
    bc_p, pooled_p, sgc_p, sgp_p, cvlast, zlast = _mix_prompt(u_p, w_groups, cw, cinit, zinit,
                                                              n_seq=batch, tm=1024, n_j=4)
    tn = wpg.shape[2]
    h2_p = _outproj_resident(bc_p, pooled_p, sgc_p, sgp_p, h_p, wco, wpg, pscale, wo.reshape(d // tn, tn, d),
                             tm=512, groups=2, name="outproj_prompt")
    (y_p,) = _ffn_stream(h2_p, g2, wg2, wu2, down_chunks(wd2, wg2), gf, tm=512, unit=2, emit_h=False,
                         name="ffn2_prompt")

    return (y_p.reshape(batch, seq, d),
            y_s.reshape(dec_batch, dec_seq, d),
            cvlast[None],
            zlast[:, 1:][None],
            new_conv_sample[None],
            new_pool_sample[None])
```

```python
import functools

import jax
import jax.numpy as jnp
from jax import lax
from jax.experimental import pallas as pl
from jax.experimental.pallas import tpu as pltpu

F32 = jnp.float32
BF16 = jnp.bfloat16

EPS = 1e-6
N_META = 16
CONV_W = 3
POOL_WINDOWS = (2, 4, 8, 16)
MAX_WIN = max(POOL_WINDOWS)
N_GROUPS = len(POOL_WINDOWS)

F32_SUBLANES = 8
MXU_COLUMNS = 256
PIECE_ROWS = 128
V7X_VMEM_LIMIT_BYTES = 60 * 1024 * 1024


def _params(semantics):
    return pltpu.CompilerParams(dimension_semantics=semantics,
                                vmem_limit_bytes=V7X_VMEM_LIMIT_BYTES)


def _dot(a, b):
    return jnp.dot(a, b, preferred_element_type=F32)


def _rms(x, g):
    ms = jnp.mean(x * x, axis=-1, keepdims=True)
    return (x * lax.rsqrt(ms + EPS)) * g


def _resident(shape, index_map):
    return pl.BlockSpec(shape, index_map, pipeline_mode=pl.Buffered(1))


def _block(ref):
    return ref[0] if len(ref.shape) == 3 else ref[...]


def _set_block(ref, value):
    if len(ref.shape) == 3:
        ref[0] = value
    else:
        ref[...] = value


def _weight(w_ref, copy_ref):
    w = _block(w_ref).astype(BF16)
    if copy_ref is not None:
        _set_block(copy_ref, w)
    return w


def _ffn_cast_body(*refs, n_parts, emit_h):
    refs = list(refs)
    x_refs = [refs.pop(0) for _ in range(n_parts)]
    g_ref, wg_ref, wu_ref, wd_ref, g2_ref, acc_ref = (refs.pop(0) for _ in range(6))
    n_ref = refs.pop(0) if emit_h else acc_ref
    wg_c, wu_c, wd_c, xn_ref = refs
    f = pl.program_id(0)

    def x():
        return jnp.concatenate([r[...].reshape(-1, r.shape[-1]) for r in x_refs], axis=0)

    @pl.when(f == 0)
    def _():
        xn_ref[...] = _rms(x(), g_ref[...]).astype(BF16)
        acc_ref[...] = jnp.zeros(acc_ref.shape, F32)

    xn = xn_ref[...]
    a = _dot(xn, _weight(wg_ref, wg_c))
    b = _dot(xn, _weight(wu_ref, wu_c))
    act = ((a * jax.nn.sigmoid(a)) * b).astype(BF16)
    acc_ref[...] += _dot(act, _weight(wd_ref, wd_c))

    @pl.when(f == pl.num_programs(0) - 1)
    def _():
        h = x() + 0.5 * acc_ref[...]
        if emit_h:
            acc_ref[...] = h
        n_ref[...] = _rms(h, g2_ref[...]).astype(n_ref.dtype)


def _ffn_cast(x_parts, g, wg, wu, wd, g2, *, tf, w_chunk, emit_h, name):
    d, d_ff = wg.shape
    part_rows = [p.size // d for p in x_parts]
    rows = sum(part_rows)
    n_f = d_ff // tf
    assert d_ff == n_f * tf and w_chunk % tf == 0 and all(r % F32_SUBLANES == 0 for r in part_rows)
    assert all(p.ndim == 2 or p.shape[1] == F32_SUBLANES for p in x_parts)
    per_chunk = w_chunk // tf
    whole = lambda shape: _resident(shape, lambda f: (0,) * len(shape))
    col_spec = pl.BlockSpec((d, tf), lambda f: (0, f))
    wd_spec = pl.BlockSpec((tf, d), lambda f: (f, 0))
    chunk_spec = pl.BlockSpec((1, d, tf), lambda f: (f // per_chunk, 0, f % per_chunk))
    gain_spec = pl.BlockSpec((1, d), lambda f: (0, 0))
    in_specs = [whole(p.shape) for p in x_parts] + [gain_spec, col_spec, col_spec, wd_spec, gain_spec]
    out_shape = [jax.ShapeDtypeStruct((rows, d), F32)]
    out_specs = [whole((rows, d))]
    if emit_h:
        out_shape.append(jax.ShapeDtypeStruct((rows, d), BF16))
        out_specs.append(whole((rows, d)))
    chunked = jax.ShapeDtypeStruct((d_ff // w_chunk, d, w_chunk), BF16)
    return pl.pallas_call(
        functools.partial(_ffn_cast_body, n_parts=len(x_parts), emit_h=emit_h),
        grid=(n_f,),
        in_specs=in_specs,
        out_specs=out_specs + [chunk_spec, chunk_spec, wd_spec],
        out_shape=out_shape + [chunked, chunked, jax.ShapeDtypeStruct(wd.shape, BF16)],
        scratch_shapes=[pltpu.VMEM((rows, d), BF16)],
        compiler_params=_params(("arbitrary",)),
        name=name,
    )(*x_parts, g, wg, wu, wd, g2)


W_SLOTS = 2
STREAM_PIECE_ROWS = 128


def _ffn_stream_body(x_ref, g_ref, wg_hbm, wu_hbm, wd_hbm, g2_ref, *refs, emit_h, units):
    refs = list(refs)
    acc_ref = refs.pop(0)
    n_ref = refs.pop(0) if emit_h else acc_ref
    xn_ref, wg_buf, wu_buf, wd_buf, sem = refs
    i = pl.program_id(0)
    n_units = len(units)

    def slot_of(u):
        return u % W_SLOTS if n_units % W_SLOTS == 0 else (i * n_units + u) % W_SLOTS

    def copy(w_hbm, w_buf, which, u):
        first, count = units[u % n_units]
        slot = slot_of(u)
        return pltpu.make_async_copy(w_hbm.at[pl.ds(first, count)], w_buf.at[slot, pl.ds(0, count)],
                                     sem.at[which, slot])

    def gate_up(u):
        return (copy(wg_hbm, wg_buf, 0, u), copy(wu_hbm, wu_buf, 1, u))

    def down(u):
        return (copy(wd_hbm, wd_buf, 2, u),)

    @pl.when(i == 0)
    def _():
        for cp in gate_up(0) + gate_up(1) + down(0):
            cp.start()
        for cp in gate_up(0):
            cp.wait()

    xn_ref[...] = _rms(x_ref[...], g_ref[...]).astype(BF16)
    pieces = [pl.ds(r, STREAM_PIECE_ROWS) for r in range(0, xn_ref.shape[0], STREAM_PIECE_ROWS)]
    for u, (_, count) in enumerate(units):
        slot = slot_of(u)
        acts = []
        for rows in pieces:
            xn = xn_ref[rows, :]
            acts.append([])
            for q in range(count):
                halves = []
                for c in range(0, wg_buf.shape[3], MXU_COLUMNS):
                    a = _dot(xn, wg_buf[slot, q, :, c:c + MXU_COLUMNS])
                    b = _dot(xn, wu_buf[slot, q, :, c:c + MXU_COLUMNS])
                    halves.append(((a * jax.nn.sigmoid(a)) * b).astype(BF16))
                acts[-1].append(jnp.concatenate(halves, axis=1))
        for cp in down(u) + gate_up(u + 1):
            cp.wait()
        for cp in gate_up(u + 2) + down(u + 1):
            cp.start()
        for rows, piece_acts in zip(pieces, acts):
            part = _dot(piece_acts[0], wd_buf[slot, 0])
            for q in range(1, count):
                part = part + _dot(piece_acts[q], wd_buf[slot, q])
            if u == 0:
                acc_ref[rows, :] = part
            else:
                acc_ref[rows, :] += part

    h = x_ref[...] + 0.5 * acc_ref[...]
    if emit_h:
        acc_ref[...] = h
    n_ref[...] = _rms(h, g2_ref[...]).astype(n_ref.dtype)

    @pl.when(i == pl.num_programs(0) - 1)
    def _():
        for cp in gate_up(n_units + 1) + down(n_units):
            cp.wait()


def _ffn_stream(x, g, wg, wu, wd, g2, *, tm, unit, emit_h, name):
    rows, d = x.shape
    n_f, _, tf = wg.shape
    n_i = rows // tm
    assert rows == n_i * tm and wu.shape == wg.shape and wd.shape == (n_f, tf, d)
    units = tuple((first, min(unit, n_f - first)) for first in range(0, n_f, unit))
    assert len(units) >= W_SLOTS
    row = lambda i: (i, 0)
    hbm = pl.BlockSpec(memory_space=pl.ANY)
    in_specs = [pl.BlockSpec((tm, d), row), pl.BlockSpec((1, d), lambda i: (0, 0)), hbm, hbm, hbm,
                pl.BlockSpec((1, d), lambda i: (0, 0))]
    out_shape = [jax.ShapeDtypeStruct((rows, d), F32)]
    out_specs = [pl.BlockSpec((tm, d), row)]
    if emit_h:
        out_shape.append(jax.ShapeDtypeStruct((rows, d), BF16))
        out_specs.append(pl.BlockSpec((tm, d), row))
    scratch = [pltpu.VMEM((tm, d), BF16),
               pltpu.VMEM((W_SLOTS, unit, d, tf), BF16), pltpu.VMEM((W_SLOTS, unit, d, tf), BF16),
               pltpu.VMEM((W_SLOTS, unit, tf, d), BF16),
               pltpu.SemaphoreType.DMA((3, W_SLOTS))]
    return pl.pallas_call(
        functools.partial(_ffn_stream_body, emit_h=emit_h, units=units),
        grid=(n_i,),
        in_specs=in_specs,
        out_specs=out_specs,
        out_shape=out_shape,
        scratch_shapes=scratch,
        compiler_params=_params(("arbitrary",)),
        name=name,
    )(x, g, wg, wu, wd, g2)


def _conv3(w_ref, cur, prev1, prev2):
    w = w_ref[...]
    y = w[0:1] * prev2
    y = y + w[1:2] * prev1
    return y + w[2:3] * cur


def _window_sums(e, g):
    p = e + pltpu.roll(e, 1, axis=0)
    for level, shift in ((1, 2), (2, 4), (3, 8)):
        p = jnp.where(g >= level, p + pltpu.roll(p, shift, axis=0), p)
    return p


def _inv_window(g):
    return jnp.where(g == 0, 0.5, jnp.where(g == 1, 0.25, jnp.where(g == 2, 0.125, 0.0625))).astype(F32)


def _mix_prompt_body(u_ref, wb_ref, wc_ref, wv_ref, wz_ref, wgc_ref, wgp_ref, cw_ref, cinit_ref,
                     zinit_ref, bc_ref, pooled_ref, sgc_ref, sgp_ref, cvlast_ref, zlast_ref,
                     cpast_ref, zpast_ref, *, tm, tiles_per_seq, n_j):
    j = pl.program_id(0)
    i = pl.program_id(1)
    t = i % tiles_per_seq
    b = i // tiles_per_seq
    g = j // (n_j // N_GROUPS)

    @pl.when(t == 0)
    def _():
        cpast_ref[...] = cinit_ref[...]
        zpast_ref[...] = zinit_ref[...]

    piece = PIECE_ROWS
    for p in range(tm // piece):
        rows = pl.ds(p * piece, piece)
        u = u_ref[rows, :]

        cv = _dot(u, wc_ref[...]) * _dot(u, wv_ref[...])
        ext = jnp.concatenate([cpast_ref[...], cv], axis=0)
        y = _conv3(cw_ref, cv, pltpu.roll(ext, 1, axis=0)[F32_SUBLANES:], pltpu.roll(ext, 2, axis=0)[F32_SUBLANES:])
        bc_ref[0, rows, :] = (_dot(u, wb_ref[...]) * y).astype(BF16)
        cpast_ref[...] = cv[piece - F32_SUBLANES:]

        z = _dot(u, wz_ref[...])
        sums = _window_sums(jnp.concatenate([zpast_ref[...], z], axis=0), g)[MAX_WIN:]
        pooled_ref[0, rows, :] = (sums * _inv_window(g) - z).astype(BF16)
        zpast_ref[...] = z[piece - MAX_WIN:]

        sgc_ref[0, rows, :] = jax.nn.sigmoid(_dot(u, wgc_ref[...]))
        sgp_ref[0, rows, :] = jax.nn.sigmoid(_dot(u, wgp_ref[...]))

    @pl.when(t == tiles_per_seq - 1)
    def _():
        cvlast_ref[b] = cpast_ref[pl.ds(F32_SUBLANES - (CONV_W - 1), CONV_W - 1), :]
        zlast_ref[b] = zpast_ref[...]


def _mix_prompt(u, w_groups, conv_w, cinit, zinit, *, n_seq, tm, n_j):
    rows, d = u.shape
    d_pool = zinit.shape[1]
    tc, tz = d // n_j, d_pool // n_j
    seq = rows // n_seq
    tiles_per_seq = seq // tm
    n_i = rows // tm
    assert seq == tiles_per_seq * tm and n_j % N_GROUPS == 0 and tm % PIECE_ROWS == 0 and PIECE_ROWS >= MAX_WIN
    blk = lambda w: pl.BlockSpec((1, tm, w), lambda j, i: (j, i, 0))
    wcol = lambda w: pl.BlockSpec((d, w), lambda j, i: (0, j))
    in_specs = ([pl.BlockSpec((tm, d), lambda j, i: (i, 0))]
                + [wcol(tc), wcol(tc), wcol(tc), wcol(tz), wcol(tc), wcol(tc)]
                + [pl.BlockSpec((CONV_W, tc), lambda j, i: (0, j)),
                   pl.BlockSpec((F32_SUBLANES, tc), lambda j, i: (0, j)),
                   pl.BlockSpec((MAX_WIN, tz), lambda j, i: (0, j))])
    out_shape = (jax.ShapeDtypeStruct((n_j, rows, tc), BF16),
                 jax.ShapeDtypeStruct((n_j, rows, tz), BF16),
                 jax.ShapeDtypeStruct((n_j, rows, tc), F32),
                 jax.ShapeDtypeStruct((n_j, rows, tc), F32),
                 jax.ShapeDtypeStruct((n_seq, CONV_W - 1, d), F32),
                 jax.ShapeDtypeStruct((n_seq, MAX_WIN, d_pool), F32))
    out_specs = (blk(tc), blk(tz), blk(tc), blk(tc),
                 pl.BlockSpec((n_seq, CONV_W - 1, tc), lambda j, i: (0, 0, j)),
                 pl.BlockSpec((n_seq, MAX_WIN, tz), lambda j, i: (0, 0, j)))
    scratch = [pltpu.VMEM((F32_SUBLANES, tc), F32), pltpu.VMEM((MAX_WIN, tz), F32)]
    return pl.pallas_call(
        functools.partial(_mix_prompt_body, tm=tm, tiles_per_seq=tiles_per_seq, n_j=n_j),
        grid=(n_j, n_i),
        in_specs=in_specs,
        out_specs=out_specs,
        out_shape=out_shape,
        scratch_shapes=scratch,
        compiler_params=_params(("arbitrary", "arbitrary")),
        name="mix_prompt",
    )(u, *w_groups, conv_w, cinit, zinit)


def _mix_sample_body(u_ref, wb_ref, wc_ref, wv_ref, wz_ref, wgc_ref, wgp_ref, cw_ref, cst_ref,
                     zst_ref, bc_ref, pooled_ref, sgc_ref, sgp_ref, cvall_ref, zall_ref,
                     wb_c, wc_c, wv_c, wz_c, wgc_c, wgp_c, *, n_seq, n_j):
    seq = F32_SUBLANES
    ns = n_seq * seq
    j = pl.program_id(0)
    g = j // (n_j // N_GROUPS)
    u = u_ref[...]
    us = u[:ns]
    tc = cvall_ref.shape[1]
    tz = zall_ref.shape[1]

    cv_all = _dot(u, _weight(wc_ref, wc_c)) * _dot(u, _weight(wv_ref, wv_c))
    cvall_ref[...] = cv_all
    cv = cv_all[:ns]
    t = lax.broadcasted_iota(jnp.int32, (ns, 1), 0) % seq
    state = cst_ref[...]
    per_token = lambda s: jnp.broadcast_to(s, (n_seq, seq, tc)).reshape(ns, tc)
    older, newer = per_token(state[:, 0:1, :]), per_token(state[:, 1:2, :])
    prev1 = jnp.where(t == 0, newer, pltpu.roll(cv, 1, axis=0))
    prev2 = jnp.where(t == 0, older, jnp.where(t == 1, newer, pltpu.roll(cv, 2, axis=0)))
    y = _conv3(cw_ref, cv, prev1, prev2)
    bc_ref[0] = (_dot(us, _weight(wb_ref, wb_c)) * y).astype(BF16)

    z_all = _dot(u, _weight(wz_ref, wz_c))
    zall_ref[...] = z_all
    z = z_all[:ns]
    per = MAX_WIN + seq
    ext = jnp.concatenate([zst_ref[...], z.reshape(n_seq, seq, tz)], axis=1)
    sums = _window_sums(ext.reshape(n_seq * per, tz), g)
    sums = sums.reshape(n_seq, per, tz)[:, MAX_WIN:, :].reshape(ns, tz)
    pooled_ref[0] = (sums * _inv_window(g) - z).astype(BF16)

    sgc_ref[0] = jax.nn.sigmoid(_dot(us, _weight(wgc_ref, wgc_c)))
    sgp_ref[0] = jax.nn.sigmoid(_dot(us, _weight(wgp_ref, wgp_c)))


def _mix_sample(u, w_in, conv_w, cst, zst, *, n_j):
    rows, d = u.shape
    n_seq, _, d_pool = zst.shape
    ns = n_seq * F32_SUBLANES
    tc, tz = d // n_j, d_pool // n_j
    assert n_j % N_GROUPS == 0 and cst.shape == (n_seq, CONV_W - 1, d) and CONV_W == 3
    assert (3 * d + d_pool) % tc == 0
    conv_blocks = d // tc
    z0 = (3 * d) // tz
    gc0 = (3 * d + d_pool) // tc
    wcol = lambda w, off: pl.BlockSpec((d, w), lambda j: (0, off + j))
    w_specs = [wcol(tc, 0), wcol(tc, conv_blocks), wcol(tc, 2 * conv_blocks), wcol(tz, z0),
               wcol(tc, gc0), wcol(tc, gc0 + conv_blocks)]
    col = lambda r, w: pl.BlockSpec((r, w), lambda j: (0, j))
    in_specs = ([_resident((rows, d), lambda j: (0, 0))] + w_specs
                + [col(CONV_W, tc),
                   pl.BlockSpec((n_seq, CONV_W - 1, tc), lambda j: (0, 0, j)),
                   pl.BlockSpec((n_seq, MAX_WIN, tz), lambda j: (0, 0, j))])
    group_widths = (d, d, d, d_pool, d, d)
    out_shape = ([jax.ShapeDtypeStruct((n_j, ns, tc), BF16),
                  jax.ShapeDtypeStruct((n_j, ns, tz), BF16),
                  jax.ShapeDtypeStruct((n_j, ns, tc), F32),
                  jax.ShapeDtypeStruct((n_j, ns, tc), F32),
                  jax.ShapeDtypeStruct((rows, d), F32),
                  jax.ShapeDtypeStruct((rows, d_pool), F32)]
                 + [jax.ShapeDtypeStruct((d, w), BF16) for w in group_widths])
    chunk = lambda w: pl.BlockSpec((1, ns, w), lambda j: (j, 0, 0))
    out_specs = ([chunk(tc), chunk(tz), chunk(tc), chunk(tc), col(rows, tc), col(rows, tz)]
                 + [col(d, tz if w == d_pool else tc) for w in group_widths])
    return pl.pallas_call(
        functools.partial(_mix_sample_body, n_seq=n_seq, n_j=n_j),
        grid=(n_j,),
        in_specs=in_specs,
        out_specs=out_specs,
        out_shape=out_shape,
        compiler_params=_params(("arbitrary",)),
        name="mix_sample",
    )(u, w_in, w_in, w_in, w_in, w_in, w_in, conv_w, cst, zst)


def _join_chunks(ref):
    return jnp.concatenate([ref[c] for c in range(ref.shape[0])], axis=1)


def _merged_block(bc, pooled, sgc, sgp, pscale, wco, wpg):
    if wco.shape[1] <= MXU_COLUMNS:
        half = bc.shape[0] // 2
        y_conv = jnp.concatenate([_dot(bc[:half], wco), _dot(bc[half:], wco)], axis=0)
    else:
        y_conv = _dot(bc, wco)
    y_pool = _dot(pooled, wpg) * pscale
    return (sgc * y_conv + sgp * y_pool).astype(BF16)


def _outproj_body(bc_ref, pooled_ref, sgc_ref, sgp_ref, h_ref, wco_ref, wpg_ref, ps_ref, wo_ref,
                  h2_ref, wco_c, wpg_c, wo_c):
    @pl.when(pl.program_id(1) == 0)
    def _():
        h2_ref[...] = h_ref[...]

    m = _merged_block(_join_chunks(bc_ref), _join_chunks(pooled_ref), sgc_ref[0], sgp_ref[0], ps_ref[...],
                      _weight(wco_ref, wco_c), _weight(wpg_ref, wpg_c))
    h2_ref[...] += _dot(m, _weight(wo_ref, wo_c))


def _outproj(bc, pooled, sgc, sgp, h, wco, wpg, pscale, wo, *, tn, name):
    bc_chunks, rows, bc_width = bc.shape
    d = bc_chunks * bc_width
    n_groups, pool_in, group_out = wpg.shape
    n_n = d // tn
    per_group = group_out // tn
    pool_chunks = pooled.shape[0] // n_groups
    assert d == n_n * tn and group_out == per_group * tn and d == n_groups * group_out
    assert sgc.shape == sgp.shape == (n_n, rows, tn) and pool_chunks * pooled.shape[2] == pool_in
    chunk_spec = lambda r: pl.BlockSpec((1, r, tn), lambda i, n: (n // per_group, 0, n % per_group))
    w_specs = [pl.BlockSpec((d, tn), lambda i, n: (0, n)), chunk_spec(pool_in),
               pl.BlockSpec((tn, d), lambda i, n: (n, 0))]
    w_shapes = [jax.ShapeDtypeStruct((n_groups, d, group_out), BF16),
                jax.ShapeDtypeStruct(wpg.shape, BF16), jax.ShapeDtypeStruct(wo.shape, BF16)]
    w_out_specs = [chunk_spec(d), w_specs[1], w_specs[2]]
    in_specs = [
        _resident((bc_chunks, rows, bc_width), lambda i, n: (0, 0, 0)),
        pl.BlockSpec((pool_chunks, rows, pooled.shape[2]), lambda i, n: (n // per_group, 0, 0)),
        pl.BlockSpec((1, rows, tn), lambda i, n: (n, 0, 0)),
        pl.BlockSpec((1, rows, tn), lambda i, n: (n, 0, 0)),
        _resident((rows, d), lambda i, n: (0, 0)),
        w_specs[0], w_specs[1],
        pl.BlockSpec((1, tn), lambda i, n: (0, n)),
        w_specs[2],
    ]
    return pl.pallas_call(
        _outproj_body,
        grid=(1, n_n),
        in_specs=in_specs,
        out_specs=[_resident((rows, d), lambda i, n: (0, 0))] + w_out_specs,
        out_shape=[jax.ShapeDtypeStruct((rows, d), F32)] + w_shapes,
        compiler_params=_params(("arbitrary", "arbitrary")),
        name=name,
    )(bc, pooled, sgc, sgp, h, wco, wpg, pscale, wo)


def _outproj_resident_body(bc_ref, pooled_ref, sgc_ref, sgp_ref, h_ref, wco_ref, wpg_ref, ps_ref, wo_ref,
                           h2_ref, *, groups):
    n = pl.program_id(1)
    tn = sgc_ref.shape[2]

    @pl.when(n == 0)
    def _():
        h2_ref[...] = h_ref[...]

    bc = _join_chunks(bc_ref)
    total = None
    for q in range(groups):
        k = n * groups + q
        m = _merged_block(bc, pooled_ref[q], sgc_ref[q], sgp_ref[q], ps_ref[:, q * tn:(q + 1) * tn],
                          wco_ref[k], wpg_ref[k])
        part = _dot(m, wo_ref[k])
        total = part if total is None else total + part
    h2_ref[...] += total


def _outproj_resident(bc, pooled, sgc, sgp, h, wco, wpg, pscale, wo, *, tm, groups, name):
    n_groups, rows, tn = sgc.shape
    d = n_groups * tn
    n_i, n_n = rows // tm, n_groups // groups
    assert rows == n_i * tm and n_groups == n_n * groups
    assert bc.shape == sgp.shape == (n_groups, rows, tn) and pooled.shape[:2] == (n_groups, rows)
    assert wco.shape == (n_groups, d, tn) and wo.shape == (n_groups, tn, d) and wpg.shape[0] == n_groups
    whole = lambda a: _resident(a.shape, lambda i, n: (0, 0, 0))
    per_step = lambda w: pl.BlockSpec((groups, tm, w), lambda i, n: (n, i, 0))
    in_specs = [
        pl.BlockSpec((n_groups, tm, tn), lambda i, n: (0, i, 0)),
        per_step(pooled.shape[2]), per_step(tn), per_step(tn),
        pl.BlockSpec((tm, d), lambda i, n: (i, 0)),
        whole(wco), whole(wpg),
        pl.BlockSpec((1, groups * tn), lambda i, n: (0, n)),
        whole(wo),
    ]
    return pl.pallas_call(
        functools.partial(_outproj_resident_body, groups=groups),
        grid=(n_i, n_n),
        in_specs=in_specs,
        out_specs=pl.BlockSpec((tm, d), lambda i, n: (i, 0)),
        out_shape=jax.ShapeDtypeStruct((rows, d), F32),
        compiler_params=_params(("arbitrary", "arbitrary")),
        name=name,
    )(bc, pooled, sgc, sgp, h, wco, wpg, pscale, wo)


def kernel(x_prompt, x_sample, state_conv, state_pool, meta_tokens, norm_ffn1, w_ffn1_gate,
           w_ffn1_up, w_ffn1_down, norm_mix, w_in, conv_w, w_conv_out, w_pool_group, pool_scale,
           w_o, norm_ffn2, w_ffn2_gate, w_ffn2_up, w_ffn2_down, norm_final):
    batch, seq, d = x_prompt.shape
    dec_batch, dec_seq, _ = x_sample.shape
    depth = w_in.shape[0]
    d_pool = state_pool.shape[-1]
    assert depth == 1 and dec_seq == F32_SUBLANES and meta_tokens.shape[0] == N_META == MAX_WIN
    assert state_conv.shape[2] == CONV_W - 1 and state_pool.shape[2] == MAX_WIN - 1
    n_s = dec_batch * dec_seq

    row = lambda v: v.reshape(1, d)
    g1, gm, g2, gf = row(norm_ffn1[0]), row(norm_mix[0]), row(norm_ffn2[0]), row(norm_final)
    pscale = row(pool_scale[0])
    cw = conv_w[0]

    h_s, u_s, wg1, wu1, wd1 = _ffn_cast((x_sample, meta_tokens.astype(F32)), g1,
                                        w_ffn1_gate[0], w_ffn1_up[0], w_ffn1_down[0], gm,
                                        tf=256, w_chunk=512, emit_h=True, name="ffn1_sample")
    zst = jnp.pad(state_pool[0], ((0, 0), (1, 0), (0, 0)))
    bc_s, pooled_s, sgc_s, sgp_s, cv_all, z_all, *w_groups = _mix_sample(u_s, w_in[0], cw, state_conv[0], zst, n_j=8)
    h2_s, wco, wpg, wo = _outproj(bc_s, pooled_s, sgc_s, sgp_s, h_s, w_conv_out[0], w_pool_group[0],
                                  pscale, w_o[0], tn=256, name="outproj_sample")
    y_s, wg2, wu2, wd2 = _ffn_cast((h2_s,), g2, w_ffn2_gate[0], w_ffn2_up[0], w_ffn2_down[0], gf,
                                   tf=256, w_chunk=512, emit_h=False, name="ffn2_sample")

    cv_s = cv_all[:n_s].reshape(dec_batch, dec_seq, d)
    new_conv_sample = cv_s[:, dec_seq - (CONV_W - 1):]
    new_pool_sample = jnp.concatenate(
        [state_pool[0][:, dec_seq:], z_all[:n_s].reshape(dec_batch, dec_seq, d_pool)], axis=1)

    cinit = jnp.pad(cv_all[n_s + N_META - (CONV_W - 1):], ((F32_SUBLANES - (CONV_W - 1), 0), (0, 0)))
    zinit = z_all[n_s:]
    xp = x_prompt.reshape(batch * seq, d)
    down_chunks = lambda wd, wg: wd.reshape(wg.shape[0], wg.shape[2], d)
    h_p, u_p = _ffn_stream(xp, g1, wg1, wu1, down_chunks(wd1, wg1), gm, tm=512, unit=2, emit_h=True,
                           name="ffn1_prompt")
    bc_p, pooled_p, sgc_p, sgp_p, cvlast, zlast = _mix_prompt(u_p, w_groups, cw, cinit, zinit,
                                                              n_seq=batch, tm=1024, n_j=4)
    tn = wpg.shape[2]
    h2_p = _outproj_resident(bc_p, pooled_p, sgc_p, sgp_p, h_p, wco, wpg, pscale, wo.reshape(d // tn, tn, d),
                             tm=512, groups=2, name="outproj_prompt")
    (y_p,) = _ffn_stream(h2_p, g2, wg2, wu2, down_chunks(wd2, wg2), gf, tm=512, unit=2, emit_h=False,
                         name="ffn2_prompt")

    return (y_p.reshape(batch, seq, d),
            y_s.reshape(dec_batch, dec_seq, d),
            cvlast[None],
            zlast[:, 1:][None],
            new_conv_sample[None],
            new_pool_sample[None])
```

```python
import functools

import jax
import jax.numpy as jnp
from jax import lax
from jax.experimental import pallas as pl
from jax.experimental.pallas import tpu as pltpu

F32 = jnp.float32
BF16 = jnp.bfloat16

EPS = 1e-6
N_META = 16
CONV_W = 3
POOL_WINDOWS = (2, 4, 8, 16)
MAX_WIN = max(POOL_WINDOWS)
N_GROUPS = len(POOL_WINDOWS)

F32_SUBLANES = 8
MXU_COLUMNS = 256
ROW_PIECES = 8
V7X_VMEM_LIMIT_BYTES = 60 * 1024 * 1024


def _params(semantics):
    return pltpu.CompilerParams(dimension_semantics=semantics,
                                vmem_limit_bytes=V7X_VMEM_LIMIT_BYTES)


def _dot(a, b):
    return jnp.dot(a, b, preferred_element_type=F32)


def _rms(x, g):
    ms = jnp.mean(x * x, axis=-1, keepdims=True)
    return (x * lax.rsqrt(ms + EPS)) * g


def _resident(shape, index_map):
    return pl.BlockSpec(shape, index_map, pipeline_mode=pl.Buffered(1))


def _block(ref):
    return ref[0] if len(ref.shape) == 3 else ref[...]


def _set_block(ref, value):
    if len(ref.shape) == 3:
        ref[0] = value
    else:
        ref[...] = value


def _weight(w_ref, copy_ref):
    w = _block(w_ref).astype(BF16)
    if copy_ref is not None:
        _set_block(copy_ref, w)
    return w


def _ffn_cast_body(*refs, n_parts, emit_h):
    refs = list(refs)
    x_refs = [refs.pop(0) for _ in range(n_parts)]
    g_ref, wg_ref, wu_ref, wd_ref, g2_ref, acc_ref = (refs.pop(0) for _ in range(6))
    n_ref = refs.pop(0) if emit_h else acc_ref
    wg_c, wu_c, wd_c, xn_ref = refs
    f = pl.program_id(0)

    def x():
        return jnp.concatenate([r[...].reshape(-1, r.shape[-1]) for r in x_refs], axis=0)

    @pl.when(f == 0)
    def _():
        xn_ref[...] = _rms(x(), g_ref[...]).astype(BF16)
        acc_ref[...] = jnp.zeros(acc_ref.shape, F32)

    xn = xn_ref[...]
    a = _dot(xn, _weight(wg_ref, wg_c))
    b = _dot(xn, _weight(wu_ref, wu_c))
    act = ((a * jax.nn.sigmoid(a)) * b).astype(BF16)
    acc_ref[...] += _dot(act, _weight(wd_ref, wd_c))

    @pl.when(f == pl.num_programs(0) - 1)
    def _():
        h = x() + 0.5 * acc_ref[...]
        if emit_h:
            acc_ref[...] = h
        n_ref[...] = _rms(h, g2_ref[...]).astype(n_ref.dtype)


def _ffn_cast(x_parts, g, wg, wu, wd, g2, *, tf, w_chunk, emit_h, name):
    d, d_ff = wg.shape
    part_rows = [p.size // d for p in x_parts]
    rows = sum(part_rows)
    n_f = d_ff // tf
    assert d_ff == n_f * tf and w_chunk % tf == 0 and all(r % F32_SUBLANES == 0 for r in part_rows)
    assert all(p.ndim == 2 or p.shape[1] == F32_SUBLANES for p in x_parts)
    per_chunk = w_chunk // tf
    whole = lambda shape: _resident(shape, lambda f: (0,) * len(shape))
    col_spec = pl.BlockSpec((d, tf), lambda f: (0, f))
    wd_spec = pl.BlockSpec((tf, d), lambda f: (f, 0))
    chunk_spec = pl.BlockSpec((1, d, tf), lambda f: (f // per_chunk, 0, f % per_chunk))
    gain_spec = pl.BlockSpec((1, d), lambda f: (0, 0))
    in_specs = [whole(p.shape) for p in x_parts] + [gain_spec, col_spec, col_spec, wd_spec, gain_spec]
    out_shape = [jax.ShapeDtypeStruct((rows, d), F32)]
    out_specs = [whole((rows, d))]
    if emit_h:
        out_shape.append(jax.ShapeDtypeStruct((rows, d), BF16))
        out_specs.append(whole((rows, d)))
    chunked = jax.ShapeDtypeStruct((d_ff // w_chunk, d, w_chunk), BF16)
    return pl.pallas_call(
        functools.partial(_ffn_cast_body, n_parts=len(x_parts), emit_h=emit_h),
        grid=(n_f,),
        in_specs=in_specs,
        out_specs=out_specs + [chunk_spec, chunk_spec, wd_spec],
        out_shape=out_shape + [chunked, chunked, jax.ShapeDtypeStruct(wd.shape, BF16)],
        scratch_shapes=[pltpu.VMEM((rows, d), BF16)],
        compiler_params=_params(("arbitrary",)),
        name=name,
    )(*x_parts, g, wg, wu, wd, g2)


W_SLOTS = 2
TAIL_PIECE_ROWS = 256


def _ffn_stream_body(x_ref, g_ref, wg_hbm, wu_hbm, wd_hbm, g2_ref, *refs, emit_h, units):
    refs = list(refs)
    acc_ref = refs.pop(0)
    n_ref = refs.pop(0) if emit_h else acc_ref
    xn_ref, wg_buf, wu_buf, wd_buf, sem = refs
    i = pl.program_id(0)
    n_units = len(units)

    def slot_of(u):
        return u % W_SLOTS if n_units % W_SLOTS == 0 else (i * n_units + u) % W_SLOTS

    def copy(w_hbm, w_buf, which, u):
        first, count = units[u % n_units]
        slot = slot_of(u)
        return pltpu.make_async_copy(w_hbm.at[pl.ds(first, count)], w_buf.at[slot, pl.ds(0, count)],
                                     sem.at[which, slot])

    def gate_up(u):
        return (copy(wg_hbm, wg_buf, 0, u), copy(wu_hbm, wu_buf, 1, u))

    def down(u):
        return (copy(wd_hbm, wd_buf, 2, u),)

    @pl.when(i == 0)
    def _():
        for cp in gate_up(0) + gate_up(1) + down(0):
            cp.start()
        for cp in gate_up(0):
            cp.wait()

    xn_ref[...] = _rms(x_ref[...], g_ref[...]).astype(BF16)
    for u, (_, count) in enumerate(units):
        slot = slot_of(u)
        xn = xn_ref[...]
        acts = []
        for q in range(count):
            halves = []
            for c in range(0, wg_buf.shape[3], MXU_COLUMNS):
                a = _dot(xn, wg_buf[slot, q, :, c:c + MXU_COLUMNS])
                b = _dot(xn, wu_buf[slot, q, :, c:c + MXU_COLUMNS])
                halves.append(((a * jax.nn.sigmoid(a)) * b).astype(BF16))
            acts.append(jnp.concatenate(halves, axis=1))
        for cp in down(u) + gate_up(u + 1):
            cp.wait()
        for cp in gate_up(u + 2) + down(u + 1):
            cp.start()
        if u < n_units - 1:
            part = _dot(acts[0], wd_buf[slot, 0])
            for q in range(1, count):
                part = part + _dot(acts[q], wd_buf[slot, q])
            if u == 0:
                acc_ref[...] = part
            else:
                acc_ref[...] += part
            continue
        tm = xn_ref.shape[0]
        for r in range(0, tm, TAIL_PIECE_ROWS):
            rows = pl.ds(r, TAIL_PIECE_ROWS)
            part = _dot(acts[0][r:r + TAIL_PIECE_ROWS], wd_buf[slot, 0])
            for q in range(1, count):
                part = part + _dot(acts[q][r:r + TAIL_PIECE_ROWS], wd_buf[slot, q])
            h = x_ref[rows, :] + 0.5 * (acc_ref[rows, :] + part)
            if emit_h:
                acc_ref[rows, :] = h
            n_ref[rows, :] = _rms(h, g2_ref[...]).astype(n_ref.dtype)

    @pl.when(i == pl.num_programs(0) - 1)
    def _():
        for cp in gate_up(n_units + 1) + down(n_units):
            cp.wait()


def _ffn_stream(x, g, wg, wu, wd, g2, *, tm, unit, emit_h, name):
    rows, d = x.shape
    n_f, _, tf = wg.shape
    n_i = rows // tm
    assert rows == n_i * tm and wu.shape == wg.shape and wd.shape == (n_f, tf, d)
    units = tuple((first, min(unit, n_f - first)) for first in range(0, n_f, unit))
    assert len(units) >= W_SLOTS
    row = lambda i: (i, 0)
    hbm = pl.BlockSpec(memory_space=pl.ANY)
    in_specs = [pl.BlockSpec((tm, d), row), pl.BlockSpec((1, d), lambda i: (0, 0)), hbm, hbm, hbm,
                pl.BlockSpec((1, d), lambda i: (0, 0))]
    out_shape = [jax.ShapeDtypeStruct((rows, d), F32)]
    out_specs = [pl.BlockSpec((tm, d), row)]
    if emit_h:
        out_shape.append(jax.ShapeDtypeStruct((rows, d), BF16))
        out_specs.append(pl.BlockSpec((tm, d), row))
    scratch = [pltpu.VMEM((tm, d), BF16),
               pltpu.VMEM((W_SLOTS, unit, d, tf), BF16), pltpu.VMEM((W_SLOTS, unit, d, tf), BF16),
               pltpu.VMEM((W_SLOTS, unit, tf, d), BF16),
               pltpu.SemaphoreType.DMA((3, W_SLOTS))]
    return pl.pallas_call(
        functools.partial(_ffn_stream_body, emit_h=emit_h, units=units),
        grid=(n_i,),
        in_specs=in_specs,
        out_specs=out_specs,
        out_shape=out_shape,
        scratch_shapes=scratch,
        compiler_params=_params(("arbitrary",)),
        name=name,
    )(x, g, wg, wu, wd, g2)


def _conv3(w_ref, cur, prev1, prev2):
    w = w_ref[...]
    y = w[0:1] * prev2
    y = y + w[1:2] * prev1
    return y + w[2:3] * cur


def _window_sums(e, g):
    p = e + pltpu.roll(e, 1, axis=0)
    for level, shift in ((1, 2), (2, 4), (3, 8)):
        p = jnp.where(g >= level, p + pltpu.roll(p, shift, axis=0), p)
    return p


def _inv_window(g):
    return jnp.where(g == 0, 0.5, jnp.where(g == 1, 0.25, jnp.where(g == 2, 0.125, 0.0625))).astype(F32)


def _mix_prompt_body(u_ref, wb_ref, wc_ref, wv_ref, wz_ref, wgc_ref, wgp_ref, cw_ref, cinit_ref,
                     zinit_ref, bc_ref, pooled_ref, sgc_ref, sgp_ref, cvlast_ref, zlast_ref,
                     cpast_ref, zpast_ref, *, tm, tiles_per_seq, n_j):
    j = pl.program_id(0)
    i = pl.program_id(1)
    t = i % tiles_per_seq
    b = i // tiles_per_seq
    g = j // (n_j // N_GROUPS)

    @pl.when(t == 0)
    def _():
        cpast_ref[...] = cinit_ref[...]
        zpast_ref[...] = zinit_ref[...]

    piece = tm // ROW_PIECES
    for p in range(ROW_PIECES):
        rows = pl.ds(p * piece, piece)
        u = u_ref[rows, :]

        cv = _dot(u, wc_ref[...]) * _dot(u, wv_ref[...])
        ext = jnp.concatenate([cpast_ref[...], cv], axis=0)
        y = _conv3(cw_ref, cv, pltpu.roll(ext, 1, axis=0)[F32_SUBLANES:], pltpu.roll(ext, 2, axis=0)[F32_SUBLANES:])
        bc_ref[0, rows, :] = (_dot(u, wb_ref[...]) * y).astype(BF16)
        cpast_ref[...] = cv[piece - F32_SUBLANES:]

        z = _dot(u, wz_ref[...])
        sums = _window_sums(jnp.concatenate([zpast_ref[...], z], axis=0), g)[MAX_WIN:]
        pooled_ref[0, rows, :] = (sums * _inv_window(g) - z).astype(BF16)
        zpast_ref[...] = z[piece - MAX_WIN:]

        sgc_ref[0, rows, :] = jax.nn.sigmoid(_dot(u, wgc_ref[...]))
        sgp_ref[0, rows, :] = jax.nn.sigmoid(_dot(u, wgp_ref[...]))

    @pl.when(t == tiles_per_seq - 1)
    def _():
        cvlast_ref[b] = cpast_ref[pl.ds(F32_SUBLANES - (CONV_W - 1), CONV_W - 1), :]
        zlast_ref[b] = zpast_ref[...]


def _mix_prompt(u, w_groups, conv_w, cinit, zinit, *, n_seq, tm, n_j):
    rows, d = u.shape
    d_pool = zinit.shape[1]
    tc, tz = d // n_j, d_pool // n_j
    seq = rows // n_seq
    tiles_per_seq = seq // tm
    n_i = rows // tm
    assert seq == tiles_per_seq * tm and n_j % N_GROUPS == 0 and tm >= MAX_WIN
    blk = lambda w: pl.BlockSpec((1, tm, w), lambda j, i: (j, i, 0))
    wcol = lambda w: pl.BlockSpec((d, w), lambda j, i: (0, j))
    in_specs = ([pl.BlockSpec((tm, d), lambda j, i: (i, 0))]
                + [wcol(tc), wcol(tc), wcol(tc), wcol(tz), wcol(tc), wcol(tc)]
                + [pl.BlockSpec((CONV_W, tc), lambda j, i: (0, j)),
                   pl.BlockSpec((F32_SUBLANES, tc), lambda j, i: (0, j)),
                   pl.BlockSpec((MAX_WIN, tz), lambda j, i: (0, j))])
    out_shape = (jax.ShapeDtypeStruct((n_j, rows, tc), BF16),
                 jax.ShapeDtypeStruct((n_j, rows, tz), BF16),
                 jax.ShapeDtypeStruct((n_j, rows, tc), F32),
                 jax.ShapeDtypeStruct((n_j, rows, tc), F32),
                 jax.ShapeDtypeStruct((n_seq, CONV_W - 1, d), F32),
                 jax.ShapeDtypeStruct((n_seq, MAX_WIN, d_pool), F32))
    out_specs = (blk(tc), blk(tz), blk(tc), blk(tc),
                 pl.BlockSpec((n_seq, CONV_W - 1, tc), lambda j, i: (0, 0, j)),
                 pl.BlockSpec((n_seq, MAX_WIN, tz), lambda j, i: (0, 0, j)))
    scratch = [pltpu.VMEM((F32_SUBLANES, tc), F32), pltpu.VMEM((MAX_WIN, tz), F32)]
    return pl.pallas_call(
        functools.partial(_mix_prompt_body, tm=tm, tiles_per_seq=tiles_per_seq, n_j=n_j),
        grid=(n_j, n_i),
        in_specs=in_specs,
        out_specs=out_specs,
        out_shape=out_shape,
        scratch_shapes=scratch,
        compiler_params=_params(("arbitrary", "arbitrary")),
        name="mix_prompt",
    )(u, *w_groups, conv_w, cinit, zinit)


def _mix_sample_body(u_ref, wb_ref, wc_ref, wv_ref, wz_ref, wgc_ref, wgp_ref, cw_ref, cst_ref,
                     zst_ref, bc_ref, pooled_ref, sgc_ref, sgp_ref, cvall_ref, zall_ref,
                     wb_c, wc_c, wv_c, wz_c, wgc_c, wgp_c, *, n_seq, n_j):
    seq = F32_SUBLANES
    ns = n_seq * seq
    j = pl.program_id(0)
    g = j // (n_j // N_GROUPS)
    u = u_ref[...]
    us = u[:ns]
    tc = cvall_ref.shape[1]
    tz = zall_ref.shape[1]

    cv_all = _dot(u, _weight(wc_ref, wc_c)) * _dot(u, _weight(wv_ref, wv_c))
    cvall_ref[...] = cv_all
    cv = cv_all[:ns]
    t = lax.broadcasted_iota(jnp.int32, (ns, 1), 0) % seq
    state = cst_ref[...]
    per_token = lambda s: jnp.broadcast_to(s, (n_seq, seq, tc)).reshape(ns, tc)
    older, newer = per_token(state[:, 0:1, :]), per_token(state[:, 1:2, :])
    prev1 = jnp.where(t == 0, newer, pltpu.roll(cv, 1, axis=0))
    prev2 = jnp.where(t == 0, older, jnp.where(t == 1, newer, pltpu.roll(cv, 2, axis=0)))
    y = _conv3(cw_ref, cv, prev1, prev2)
    bc_ref[0] = (_dot(us, _weight(wb_ref, wb_c)) * y).astype(BF16)

    z_all = _dot(u, _weight(wz_ref, wz_c))
    zall_ref[...] = z_all
    z = z_all[:ns]
    per = MAX_WIN + seq
    ext = jnp.concatenate([zst_ref[...], z.reshape(n_seq, seq, tz)], axis=1)
    sums = _window_sums(ext.reshape(n_seq * per, tz), g)
    sums = sums.reshape(n_seq, per, tz)[:, MAX_WIN:, :].reshape(ns, tz)
    pooled_ref[0] = (sums * _inv_window(g) - z).astype(BF16)

    sgc_ref[0] = jax.nn.sigmoid(_dot(us, _weight(wgc_ref, wgc_c)))
    sgp_ref[0] = jax.nn.sigmoid(_dot(us, _weight(wgp_ref, wgp_c)))


def _mix_sample(u, w_in, conv_w, cst, zst, *, n_j):
    rows, d = u.shape
    n_seq, _, d_pool = zst.shape
    ns = n_seq * F32_SUBLANES
    tc, tz = d // n_j, d_pool // n_j
    assert n_j % N_GROUPS == 0 and cst.shape == (n_seq, CONV_W - 1, d) and CONV_W == 3
    assert (3 * d + d_pool) % tc == 0
    conv_blocks = d // tc
    z0 = (3 * d) // tz
    gc0 = (3 * d + d_pool) // tc
    wcol = lambda w, off: pl.BlockSpec((d, w), lambda j: (0, off + j))
    w_specs = [wcol(tc, 0), wcol(tc, conv_blocks), wcol(tc, 2 * conv_blocks), wcol(tz, z0),
               wcol(tc, gc0), wcol(tc, gc0 + conv_blocks)]
    col = lambda r, w: pl.BlockSpec((r, w), lambda j: (0, j))
    in_specs = ([_resident((rows, d), lambda j: (0, 0))] + w_specs
                + [col(CONV_W, tc),
                   pl.BlockSpec((n_seq, CONV_W - 1, tc), lambda j: (0, 0, j)),
                   pl.BlockSpec((n_seq, MAX_WIN, tz), lambda j: (0, 0, j))])
    group_widths = (d, d, d, d_pool, d, d)
    out_shape = ([jax.ShapeDtypeStruct((n_j, ns, tc), BF16),
                  jax.ShapeDtypeStruct((n_j, ns, tz), BF16),
                  jax.ShapeDtypeStruct((n_j, ns, tc), F32),
                  jax.ShapeDtypeStruct((n_j, ns, tc), F32),
                  jax.ShapeDtypeStruct((rows, d), F32),
                  jax.ShapeDtypeStruct((rows, d_pool), F32)]
                 + [jax.ShapeDtypeStruct((d, w), BF16) for w in group_widths])
    chunk = lambda w: pl.BlockSpec((1, ns, w), lambda j: (j, 0, 0))
    out_specs = ([chunk(tc), chunk(tz), chunk(tc), chunk(tc), col(rows, tc), col(rows, tz)]
                 + [col(d, tz if w == d_pool else tc) for w in group_widths])
    return pl.pallas_call(
        functools.partial(_mix_sample_body, n_seq=n_seq, n_j=n_j),
        grid=(n_j,),
        in_specs=in_specs,
        out_specs=out_specs,
        out_shape=out_shape,
        compiler_params=_params(("arbitrary",)),
        name="mix_sample",
    )(u, w_in, w_in, w_in, w_in, w_in, w_in, conv_w, cst, zst)


def _join_chunks(ref):
    return jnp.concatenate([ref[c] for c in range(ref.shape[0])], axis=1)


def _merged_block(bc, pooled, sgc, sgp, pscale, wco, wpg):
    if wco.shape[1] <= MXU_COLUMNS:
        half = bc.shape[0] // 2
        y_conv = jnp.concatenate([_dot(bc[:half], wco), _dot(bc[half:], wco)], axis=0)
    else:
        y_conv = _dot(bc, wco)
    y_pool = _dot(pooled, wpg) * pscale
    return (sgc * y_conv + sgp * y_pool).astype(BF16)


def _outproj_body(bc_ref, pooled_ref, sgc_ref, sgp_ref, h_ref, wco_ref, wpg_ref, ps_ref, wo_ref,
                  h2_ref, wco_c, wpg_c, wo_c):
    @pl.when(pl.program_id(1) == 0)
    def _():
        h2_ref[...] = h_ref[...]

    m = _merged_block(_join_chunks(bc_ref), _join_chunks(pooled_ref), sgc_ref[0], sgp_ref[0], ps_ref[...],
                      _weight(wco_ref, wco_c), _weight(wpg_ref, wpg_c))
    h2_ref[...] += _dot(m, _weight(wo_ref, wo_c))


def _outproj(bc, pooled, sgc, sgp, h, wco, wpg, pscale, wo, *, tn, name):
    bc_chunks, rows, bc_width = bc.shape
    d = bc_chunks * bc_width
    n_groups, pool_in, group_out = wpg.shape
    n_n = d // tn
    per_group = group_out // tn
    pool_chunks = pooled.shape[0] // n_groups
    assert d == n_n * tn and group_out == per_group * tn and d == n_groups * group_out
    assert sgc.shape == sgp.shape == (n_n, rows, tn) and pool_chunks * pooled.shape[2] == pool_in
    chunk_spec = lambda r: pl.BlockSpec((1, r, tn), lambda i, n: (n // per_group, 0, n % per_group))
    w_specs = [pl.BlockSpec((d, tn), lambda i, n: (0, n)), chunk_spec(pool_in),
               pl.BlockSpec((tn, d), lambda i, n: (n, 0))]
    w_shapes = [jax.ShapeDtypeStruct((n_groups, d, group_out), BF16),
                jax.ShapeDtypeStruct(wpg.shape, BF16), jax.ShapeDtypeStruct(wo.shape, BF16)]
    w_out_specs = [chunk_spec(d), w_specs[1], w_specs[2]]
    in_specs = [
        _resident((bc_chunks, rows, bc_width), lambda i, n: (0, 0, 0)),
        pl.BlockSpec((pool_chunks, rows, pooled.shape[2]), lambda i, n: (n // per_group, 0, 0)),
        pl.BlockSpec((1, rows, tn), lambda i, n: (n, 0, 0)),
        pl.BlockSpec((1, rows, tn), lambda i, n: (n, 0, 0)),
        _resident((rows, d), lambda i, n: (0, 0)),
        w_specs[0], w_specs[1],
        pl.BlockSpec((1, tn), lambda i, n: (0, n)),
        w_specs[2],
    ]
    return pl.pallas_call(
        _outproj_body,
        grid=(1, n_n),
        in_specs=in_specs,
        out_specs=[_resident((rows, d), lambda i, n: (0, 0))] + w_out_specs,
        out_shape=[jax.ShapeDtypeStruct((rows, d), F32)] + w_shapes,
        compiler_params=_params(("arbitrary", "arbitrary")),
        name=name,
    )(bc, pooled, sgc, sgp, h, wco, wpg, pscale, wo)


def _outproj_resident_body(bc_ref, pooled_ref, sgc_ref, sgp_ref, h_ref, wco_ref, wpg_ref, ps_ref, wo_ref,
                           h2_ref, *, groups):
    n = pl.program_id(1)
    tn = sgc_ref.shape[2]

    @pl.when(n == 0)
    def _():
        h2_ref[...] = h_ref[...]

    bc = _join_chunks(bc_ref)
    total = None
    for q in range(groups):
        k = n * groups + q
        m = _merged_block(bc, pooled_ref[q], sgc_ref[q], sgp_ref[q], ps_ref[:, q * tn:(q + 1) * tn],
                          wco_ref[k], wpg_ref[k])
        part = _dot(m, wo_ref[k])
        total = part if total is None else total + part
    h2_ref[...] += total


def _outproj_resident(bc, pooled, sgc, sgp, h, wco, wpg, pscale, wo, *, tm, groups, name):
    n_groups, rows, tn = sgc.shape
    d = n_groups * tn
    n_i, n_n = rows // tm, n_groups // groups
    assert rows == n_i * tm and n_groups == n_n * groups
    assert bc.shape == sgp.shape == (n_groups, rows, tn) and pooled.shape[:2] == (n_groups, rows)
    assert wco.shape == (n_groups, d, tn) and wo.shape == (n_groups, tn, d) and wpg.shape[0] == n_groups
    whole = lambda a: _resident(a.shape, lambda i, n: (0, 0, 0))
    per_step = lambda w: pl.BlockSpec((groups, tm, w), lambda i, n: (n, i, 0))
    in_specs = [
        pl.BlockSpec((n_groups, tm, tn), lambda i, n: (0, i, 0)),
        per_step(pooled.shape[2]), per_step(tn), per_step(tn),
        pl.BlockSpec((tm, d), lambda i, n: (i, 0)),
        whole(wco), whole(wpg),
        pl.BlockSpec((1, groups * tn), lambda i, n: (0, n)),
        whole(wo),
    ]
    return pl.pallas_call(
        functools.partial(_outproj_resident_body, groups=groups),
        grid=(n_i, n_n),
        in_specs=in_specs,
        out_specs=pl.BlockSpec((tm, d), lambda i, n: (i, 0)),
        out_shape=jax.ShapeDtypeStruct((rows, d), F32),
        compiler_params=_params(("arbitrary", "arbitrary")),
        name=name,
    )(bc, pooled, sgc, sgp, h, wco, wpg, pscale, wo)


def kernel(x_prompt, x_sample, state_conv, state_pool, meta_tokens, norm_ffn1, w_ffn1_gate,
           w_ffn1_up, w_ffn1_down, norm_mix, w_in, conv_w, w_conv_out, w_pool_group, pool_scale,
           w_o, norm_ffn2, w_ffn2_gate, w_ffn2_up, w_ffn2_down, norm_final):
    batch, seq, d = x_prompt.shape
    dec_batch, dec_seq, _ = x_sample.shape
    depth = w_in.shape[0]
    d_pool = state_pool.shape[-1]
    assert depth == 1 and dec_seq == F32_SUBLANES and meta_tokens.shape[0] == N_META == MAX_WIN
    assert state_conv.shape[2] == CONV_W - 1 and state_pool.shape[2] == MAX_WIN - 1
    n_s = dec_batch * dec_seq

    row = lambda v: v.reshape(1, d)
    g1, gm, g2, gf = row(norm_ffn1[0]), row(norm_mix[0]), row(norm_ffn2[0]), row(norm_final)
    pscale = row(pool_scale[0])
    cw = conv_w[0]

    h_s, u_s, wg1, wu1, wd1 = _ffn_cast((x_sample, meta_tokens.astype(F32)), g1,
                                        w_ffn1_gate[0], w_ffn1_up[0], w_ffn1_down[0], gm,
                                        tf=256, w_chunk=512, emit_h=True, name="ffn1_sample")
    zst = jnp.pad(state_pool[0], ((0, 0), (1, 0), (0, 0)))
    bc_s, pooled_s, sgc_s, sgp_s, cv_all, z_all, *w_groups = _mix_sample(u_s, w_in[0], cw, state_conv[0], zst, n_j=8)
    h2_s, wco, wpg, wo = _outproj(bc_s, pooled_s, sgc_s, sgp_s, h_s, w_conv_out[0], w_pool_group[0],
                                  pscale, w_o[0], tn=256, name="outproj_sample")
    y_s, wg2, wu2, wd2 = _ffn_cast((h2_s,), g2, w_ffn2_gate[0], w_ffn2_up[0], w_ffn2_down[0], gf,
                                   tf=256, w_chunk=512, emit_h=False, name="ffn2_sample")

    cv_s = cv_all[:n_s].reshape(dec_batch, dec_seq, d)
    new_conv_sample = cv_s[:, dec_seq - (CONV_W - 1):]
    new_pool_sample = jnp.concatenate(
        [state_pool[0][:, dec_seq:], z_all[:n_s].reshape(dec_batch, dec_seq, d_pool)], axis=1)

    cinit = jnp.pad(cv_all[n_s + N_META - (CONV_W - 1):], ((F32_SUBLANES - (CONV_W - 1), 0), (0, 0)))
    zinit = z_all[n_s:]
    xp = x_prompt.reshape(batch * seq, d)
    down_chunks = lambda wd, wg: wd.reshape(wg.shape[0], wg.shape[2], d)
    h_p, u_p = _ffn_stream(xp, g1, wg1, wu1, down_chunks(wd1, wg1), gm, tm=512, unit=2, emit_h=True,
                           name="ffn1_prompt")
    bc_p, pooled_p, sgc_p, sgp_p, cvlast, zlast = _mix_prompt(u_p, w_groups, cw, cinit, zinit,
                                                              n_seq=batch, tm=1024, n_j=4)
    tn = wpg.shape[2]
    h2_p = _outproj_resident(bc_p, pooled_p, sgc_p, sgp_p, h_p, wco, wpg, pscale, wo.reshape(d // tn, tn, d),
                             tm=512, groups=2, name="outproj_prompt")
    (y_p,) = _ffn_stream(h2_p, g2, wg2, wu2, down_chunks(wd2, wg2), gf, tm=512, unit=2, emit_h=False,
                         name="ffn2_prompt")

    return (y_p.reshape(batch, seq, d),
            y_s.reshape(dec_batch, dec_seq, d),
            cvlast[None],
            zlast[:, 1:][None],
            new_conv_sample[None],
            new_pool_sample[None])
```

```python
import functools

import jax
import jax.numpy as jnp
from jax import lax
from jax.experimental import pallas as pl
from jax.experimental.pallas import tpu as pltpu

F32 = jnp.float32
BF16 = jnp.bfloat16

EPS = 1e-6
N_META = 16
CONV_W = 3
POOL_WINDOWS = (2, 4, 8, 16)
MAX_WIN = max(POOL_WINDOWS)
N_GROUPS = len(POOL_WINDOWS)

F32_SUBLANES = 8
MXU_COLUMNS = 256
ROW_PIECES = 8
V7X_VMEM_LIMIT_BYTES = 62 * 1024 * 1024


def _params(semantics):
    return pltpu.CompilerParams(dimension_semantics=semantics,
                                vmem_limit_bytes=V7X_VMEM_LIMIT_BYTES)


def _dot(a, b):
    return jnp.dot(a, b, preferred_element_type=F32)


def _rms(x, g):
    ms = jnp.mean(x * x, axis=-1, keepdims=True)
    return (x * lax.rsqrt(ms + EPS)) * g


def _resident(shape, index_map):
    return pl.BlockSpec(shape, index_map, pipeline_mode=pl.Buffered(1))


def _block(ref):
    return ref[0] if len(ref.shape) == 3 else ref[...]


def _set_block(ref, value):
    if len(ref.shape) == 3:
        ref[0] = value
    else:
        ref[...] = value


def _weight(w_ref, copy_ref):
    w = _block(w_ref).astype(BF16)
    if copy_ref is not None:
        _set_block(copy_ref, w)
    return w


def _ffn_cast_body(*refs, n_parts, emit_h):
    refs = list(refs)
    x_refs = [refs.pop(0) for _ in range(n_parts)]
    g_ref, wg_ref, wu_ref, wd_ref, g2_ref, acc_ref = (refs.pop(0) for _ in range(6))
    n_ref = refs.pop(0) if emit_h else acc_ref
    wg_c, wu_c, wd_c, xn_ref = refs
    f = pl.program_id(0)

    def x():
        return jnp.concatenate([r[...].reshape(-1, r.shape[-1]) for r in x_refs], axis=0)

    @pl.when(f == 0)
    def _():
        xn_ref[...] = _rms(x(), g_ref[...]).astype(BF16)
        acc_ref[...] = jnp.zeros(acc_ref.shape, F32)

    xn = xn_ref[...]
    a = _dot(xn, _weight(wg_ref, wg_c))
    b = _dot(xn, _weight(wu_ref, wu_c))
    act = ((a * jax.nn.sigmoid(a)) * b).astype(BF16)
    acc_ref[...] += _dot(act, _weight(wd_ref, wd_c))

    @pl.when(f == pl.num_programs(0) - 1)
    def _():
        h = x() + 0.5 * acc_ref[...]
        if emit_h:
            acc_ref[...] = h
        n_ref[...] = _rms(h, g2_ref[...]).astype(n_ref.dtype)


def _ffn_cast(x_parts, g, wg, wu, wd, g2, *, tf, w_chunk, emit_h, name):
    d, d_ff = wg.shape
    part_rows = [p.size // d for p in x_parts]
    rows = sum(part_rows)
    n_f = d_ff // tf
    assert d_ff == n_f * tf and w_chunk % tf == 0 and all(r % F32_SUBLANES == 0 for r in part_rows)
    assert all(p.ndim == 2 or p.shape[1] == F32_SUBLANES for p in x_parts)
    per_chunk = w_chunk // tf
    whole = lambda shape: _resident(shape, lambda f: (0,) * len(shape))
    col_spec = pl.BlockSpec((d, tf), lambda f: (0, f))
    wd_spec = pl.BlockSpec((tf, d), lambda f: (f, 0))
    chunk_spec = pl.BlockSpec((1, d, tf), lambda f: (f // per_chunk, 0, f % per_chunk))
    gain_spec = pl.BlockSpec((1, d), lambda f: (0, 0))
    in_specs = [whole(p.shape) for p in x_parts] + [gain_spec, col_spec, col_spec, wd_spec, gain_spec]
    out_shape = [jax.ShapeDtypeStruct((rows, d), F32)]
    out_specs = [whole((rows, d))]
    if emit_h:
        out_shape.append(jax.ShapeDtypeStruct((rows, d), BF16))
        out_specs.append(whole((rows, d)))
    chunked = jax.ShapeDtypeStruct((d_ff // w_chunk, d, w_chunk), BF16)
    return pl.pallas_call(
        functools.partial(_ffn_cast_body, n_parts=len(x_parts), emit_h=emit_h),
        grid=(n_f,),
        in_specs=in_specs,
        out_specs=out_specs + [chunk_spec, chunk_spec, wd_spec],
        out_shape=out_shape + [chunked, chunked, jax.ShapeDtypeStruct(wd.shape, BF16)],
        scratch_shapes=[pltpu.VMEM((rows, d), BF16)],
        compiler_params=_params(("arbitrary",)),
        name=name,
    )(*x_parts, g, wg, wu, wd, g2)


W_SLOTS = 2


def _ffn_stream_body(x_ref, g_ref, wg_hbm, wu_hbm, wd_hbm, g2_ref, *refs, emit_h, units):
    refs = list(refs)
    acc_ref = refs.pop(0)
    n_ref = refs.pop(0) if emit_h else acc_ref
    xn_ref, wg_buf, wu_buf, wd_buf, sem = refs
    i = pl.program_id(0)
    n_units = len(units)

    def slot_of(u):
        return u % W_SLOTS if n_units % W_SLOTS == 0 else (i * n_units + u) % W_SLOTS

    def copy(w_hbm, w_buf, which, u):
        first, count = units[u % n_units]
        slot = slot_of(u)
        return pltpu.make_async_copy(w_hbm.at[pl.ds(first, count)], w_buf.at[slot, pl.ds(0, count)],
                                     sem.at[which, slot])

    def gate_up(u):
        return (copy(wg_hbm, wg_buf, 0, u), copy(wu_hbm, wu_buf, 1, u))

    def down(u):
        return (copy(wd_hbm, wd_buf, 2, u),)

    @pl.when(i == 0)
    def _():
        for cp in gate_up(0) + gate_up(1) + down(0):
            cp.start()
        for cp in gate_up(0):
            cp.wait()

    xn_ref[...] = _rms(x_ref[...], g_ref[...]).astype(BF16)
    for u, (_, count) in enumerate(units):
        slot = slot_of(u)
        xn = xn_ref[...]
        acts = []
        for q in range(count):
            halves = []
            for c in range(0, wg_buf.shape[3], MXU_COLUMNS):
                a = _dot(xn, wg_buf[slot, q, :, c:c + MXU_COLUMNS])
                b = _dot(xn, wu_buf[slot, q, :, c:c + MXU_COLUMNS])
                halves.append(((a * jax.nn.sigmoid(a)) * b).astype(BF16))
            acts.append(jnp.concatenate(halves, axis=1))
        for cp in down(u) + gate_up(u + 1):
            cp.wait()
        for cp in gate_up(u + 2) + down(u + 1):
            cp.start()
        part = _dot(acts[0], wd_buf[slot, 0])
        for q in range(1, count):
            part = part + _dot(acts[q], wd_buf[slot, q])
        if u == 0:
            acc_ref[...] = part
        else:
            acc_ref[...] += part

    h = x_ref[...] + 0.5 * acc_ref[...]
    if emit_h:
        acc_ref[...] = h
    n_ref[...] = _rms(h, g2_ref[...]).astype(n_ref.dtype)

    @pl.when(i == pl.num_programs(0) - 1)
    def _():
        for cp in gate_up(n_units + 1) + down(n_units):
            cp.wait()


def _ffn_stream(x, g, wg, wu, wd, g2, *, tm, unit, emit_h, name):
    rows, d = x.shape
    n_f, _, tf = wg.shape
    n_i = rows // tm
    assert rows == n_i * tm and wu.shape == wg.shape and wd.shape == (n_f, tf, d)
    units = tuple((first, min(unit, n_f - first)) for first in range(0, n_f, unit))
    assert len(units) >= W_SLOTS
    row = lambda i: (i, 0)
    hbm = pl.BlockSpec(memory_space=pl.ANY)
    in_specs = [pl.BlockSpec((tm, d), row), pl.BlockSpec((1, d), lambda i: (0, 0)), hbm, hbm, hbm,
                pl.BlockSpec((1, d), lambda i: (0, 0))]
    out_shape = [jax.ShapeDtypeStruct((rows, d), F32)]
    out_specs = [pl.BlockSpec((tm, d), row)]
    if emit_h:
        out_shape.append(jax.ShapeDtypeStruct((rows, d), BF16))
        out_specs.append(pl.BlockSpec((tm, d), row))
    scratch = [pltpu.VMEM((tm, d), BF16),
               pltpu.VMEM((W_SLOTS, unit, d, tf), BF16), pltpu.VMEM((W_SLOTS, unit, d, tf), BF16),
               pltpu.VMEM((W_SLOTS, unit, tf, d), BF16),
               pltpu.SemaphoreType.DMA((3, W_SLOTS))]
    return pl.pallas_call(
        functools.partial(_ffn_stream_body, emit_h=emit_h, units=units),
        grid=(n_i,),
        in_specs=in_specs,
        out_specs=out_specs,
        out_shape=out_shape,
        scratch_shapes=scratch,
        compiler_params=_params(("arbitrary",)),
        name=name,
    )(x, g, wg, wu, wd, g2)


def _conv3(w_ref, cur, prev1, prev2):
    w = w_ref[...]
    y = w[0:1] * prev2
    y = y + w[1:2] * prev1
    return y + w[2:3] * cur


def _window_sums(e, g):
    p = e + pltpu.roll(e, 1, axis=0)
    for level, shift in ((1, 2), (2, 4), (3, 8)):
        p = jnp.where(g >= level, p + pltpu.roll(p, shift, axis=0), p)
    return p


def _inv_window(g):
    return jnp.where(g == 0, 0.5, jnp.where(g == 1, 0.25, jnp.where(g == 2, 0.125, 0.0625))).astype(F32)


def _mix_prompt_body(u_ref, wb_ref, wc_ref, wv_ref, wz_ref, wgc_ref, wgp_ref, cw_ref, cinit_ref,
                     zinit_ref, bc_ref, pooled_ref, sgc_ref, sgp_ref, cvlast_ref, zlast_ref,
                     cpast_ref, zpast_ref, *, tm, tiles_per_seq, n_j):
    j = pl.program_id(0)
    i = pl.program_id(1)
    t = i % tiles_per_seq
    b = i // tiles_per_seq
    g = j // (n_j // N_GROUPS)

    @pl.when(t == 0)
    def _():
        cpast_ref[...] = cinit_ref[...]
        zpast_ref[...] = zinit_ref[...]

    piece = tm // ROW_PIECES
    for p in range(ROW_PIECES):
        rows = pl.ds(p * piece, piece)
        u = u_ref[rows, :]

        cv = _dot(u, wc_ref[...]) * _dot(u, wv_ref[...])
        ext = jnp.concatenate([cpast_ref[...], cv], axis=0)
        y = _conv3(cw_ref, cv, pltpu.roll(ext, 1, axis=0)[F32_SUBLANES:], pltpu.roll(ext, 2, axis=0)[F32_SUBLANES:])
        bc_ref[0, rows, :] = (_dot(u, wb_ref[...]) * y).astype(BF16)
        cpast_ref[...] = cv[piece - F32_SUBLANES:]

        z = _dot(u, wz_ref[...])
        sums = _window_sums(jnp.concatenate([zpast_ref[...], z], axis=0), g)[MAX_WIN:]
        pooled_ref[0, rows, :] = (sums * _inv_window(g) - z).astype(BF16)
        zpast_ref[...] = z[piece - MAX_WIN:]

        sgc_ref[0, rows, :] = jax.nn.sigmoid(_dot(u, wgc_ref[...]))
        sgp_ref[0, rows, :] = jax.nn.sigmoid(_dot(u, wgp_ref[...]))

    @pl.when(t == tiles_per_seq - 1)
    def _():
        cvlast_ref[b] = cpast_ref[pl.ds(F32_SUBLANES - (CONV_W - 1), CONV_W - 1), :]
        zlast_ref[b] = zpast_ref[...]


def _mix_prompt(u, w_groups, conv_w, cinit, zinit, *, n_seq, tm, n_j):
    rows, d = u.shape
    d_pool = zinit.shape[1]
    tc, tz = d // n_j, d_pool // n_j
    seq = rows // n_seq
    tiles_per_seq = seq // tm
    n_i = rows // tm
    assert seq == tiles_per_seq * tm and n_j % N_GROUPS == 0 and tm >= MAX_WIN
    blk = lambda w: pl.BlockSpec((1, tm, w), lambda j, i: (j, i, 0))
    wcol = lambda w: pl.BlockSpec((d, w), lambda j, i: (0, j))
    in_specs = ([pl.BlockSpec((tm, d), lambda j, i: (i, 0))]
                + [wcol(tc), wcol(tc), wcol(tc), wcol(tz), wcol(tc), wcol(tc)]
                + [pl.BlockSpec((CONV_W, tc), lambda j, i: (0, j)),
                   pl.BlockSpec((F32_SUBLANES, tc), lambda j, i: (0, j)),
                   pl.BlockSpec((MAX_WIN, tz), lambda j, i: (0, j))])
    out_shape = (jax.ShapeDtypeStruct((n_j, rows, tc), BF16),
                 jax.ShapeDtypeStruct((n_j, rows, tz), BF16),
                 jax.ShapeDtypeStruct((n_j, rows, tc), F32),
                 jax.ShapeDtypeStruct((n_j, rows, tc), F32),
                 jax.ShapeDtypeStruct((n_seq, CONV_W - 1, d), F32),
                 jax.ShapeDtypeStruct((n_seq, MAX_WIN, d_pool), F32))
    out_specs = (blk(tc), blk(tz), blk(tc), blk(tc),
                 pl.BlockSpec((n_seq, CONV_W - 1, tc), lambda j, i: (0, 0, j)),
                 pl.BlockSpec((n_seq, MAX_WIN, tz), lambda j, i: (0, 0, j)))
    scratch = [pltpu.VMEM((F32_SUBLANES, tc), F32), pltpu.VMEM((MAX_WIN, tz), F32)]
    return pl.pallas_call(
        functools.partial(_mix_prompt_body, tm=tm, tiles_per_seq=tiles_per_seq, n_j=n_j),
        grid=(n_j, n_i),
        in_specs=in_specs,
        out_specs=out_specs,
        out_shape=out_shape,
        scratch_shapes=scratch,
        compiler_params=_params(("arbitrary", "arbitrary")),
        name="mix_prompt",
    )(u, *w_groups, conv_w, cinit, zinit)


def _mix_sample_body(u_ref, wb_ref, wc_ref, wv_ref, wz_ref, wgc_ref, wgp_ref, cw_ref, cst_ref,
                     zst_ref, bc_ref, pooled_ref, sgc_ref, sgp_ref, cvall_ref, zall_ref,
                     wb_c, wc_c, wv_c, wz_c, wgc_c, wgp_c, *, n_seq, n_j):
    seq = F32_SUBLANES
    ns = n_seq * seq
    j = pl.program_id(0)
    g = j // (n_j // N_GROUPS)
    u = u_ref[...]
    us = u[:ns]
    tc = cvall_ref.shape[1]
    tz = zall_ref.shape[1]

    cv_all = _dot(u, _weight(wc_ref, wc_c)) * _dot(u, _weight(wv_ref, wv_c))
    cvall_ref[...] = cv_all
    cv = cv_all[:ns]
    t = lax.broadcasted_iota(jnp.int32, (ns, 1), 0) % seq
    state = cst_ref[...]
    per_token = lambda s: jnp.broadcast_to(s, (n_seq, seq, tc)).reshape(ns, tc)
    older, newer = per_token(state[:, 0:1, :]), per_token(state[:, 1:2, :])
    prev1 = jnp.where(t == 0, newer, pltpu.roll(cv, 1, axis=0))
    prev2 = jnp.where(t == 0, older, jnp.where(t == 1, newer, pltpu.roll(cv, 2, axis=0)))
    y = _conv3(cw_ref, cv, prev1, prev2)
    bc_ref[0] = (_dot(us, _weight(wb_ref, wb_c)) * y).astype(BF16)

    z_all = _dot(u, _weight(wz_ref, wz_c))
    zall_ref[...] = z_all
    z = z_all[:ns]
    per = MAX_WIN + seq
    ext = jnp.concatenate([zst_ref[...], z.reshape(n_seq, seq, tz)], axis=1)
    sums = _window_sums(ext.reshape(n_seq * per, tz), g)
    sums = sums.reshape(n_seq, per, tz)[:, MAX_WIN:, :].reshape(ns, tz)
    pooled_ref[0] = (sums * _inv_window(g) - z).astype(BF16)

    sgc_ref[0] = jax.nn.sigmoid(_dot(us, _weight(wgc_ref, wgc_c)))
    sgp_ref[0] = jax.nn.sigmoid(_dot(us, _weight(wgp_ref, wgp_c)))


def _mix_sample(u, w_in, conv_w, cst, zst, *, n_j):
    rows, d = u.shape
    n_seq, _, d_pool = zst.shape
    ns = n_seq * F32_SUBLANES
    tc, tz = d // n_j, d_pool // n_j
    assert n_j % N_GROUPS == 0 and cst.shape == (n_seq, CONV_W - 1, d) and CONV_W == 3
    assert (3 * d + d_pool) % tc == 0
    conv_blocks = d // tc
    z0 = (3 * d) // tz
    gc0 = (3 * d + d_pool) // tc
    wcol = lambda w, off: pl.BlockSpec((d, w), lambda j: (0, off + j))
    w_specs = [wcol(tc, 0), wcol(tc, conv_blocks), wcol(tc, 2 * conv_blocks), wcol(tz, z0),
               wcol(tc, gc0), wcol(tc, gc0 + conv_blocks)]
    col = lambda r, w: pl.BlockSpec((r, w), lambda j: (0, j))
    in_specs = ([_resident((rows, d), lambda j: (0, 0))] + w_specs
                + [col(CONV_W, tc),
                   pl.BlockSpec((n_seq, CONV_W - 1, tc), lambda j: (0, 0, j)),
                   pl.BlockSpec((n_seq, MAX_WIN, tz), lambda j: (0, 0, j))])
    group_widths = (d, d, d, d_pool, d, d)
    out_shape = ([jax.ShapeDtypeStruct((n_j, ns, tc), BF16),
                  jax.ShapeDtypeStruct((n_j, ns, tz), BF16),
                  jax.ShapeDtypeStruct((n_j, ns, tc), F32),
                  jax.ShapeDtypeStruct((n_j, ns, tc), F32),
                  jax.ShapeDtypeStruct((rows, d), F32),
                  jax.ShapeDtypeStruct((rows, d_pool), F32)]
                 + [jax.ShapeDtypeStruct((d, w), BF16) for w in group_widths])
    chunk = lambda w: pl.BlockSpec((1, ns, w), lambda j: (j, 0, 0))
    out_specs = ([chunk(tc), chunk(tz), chunk(tc), chunk(tc), col(rows, tc), col(rows, tz)]
                 + [col(d, tz if w == d_pool else tc) for w in group_widths])
    return pl.pallas_call(
        functools.partial(_mix_sample_body, n_seq=n_seq, n_j=n_j),
        grid=(n_j,),
        in_specs=in_specs,
        out_specs=out_specs,
        out_shape=out_shape,
        compiler_params=_params(("arbitrary",)),
        name="mix_sample",
    )(u, w_in, w_in, w_in, w_in, w_in, w_in, conv_w, cst, zst)


def _join_chunks(ref):
    return jnp.concatenate([ref[c] for c in range(ref.shape[0])], axis=1)


def _merged_block(bc, pooled, sgc, sgp, pscale, wco, wpg):
    if wco.shape[1] <= MXU_COLUMNS:
        half = bc.shape[0] // 2
        y_conv = jnp.concatenate([_dot(bc[:half], wco), _dot(bc[half:], wco)], axis=0)
    else:
        y_conv = _dot(bc, wco)
    y_pool = _dot(pooled, wpg) * pscale
    return (sgc * y_conv + sgp * y_pool).astype(BF16)


def _outproj_body(bc_ref, pooled_ref, sgc_ref, sgp_ref, h_ref, wco_ref, wpg_ref, ps_ref, wo_ref,
                  h2_ref, wco_c, wpg_c, wo_c):
    @pl.when(pl.program_id(1) == 0)
    def _():
        h2_ref[...] = h_ref[...]

    m = _merged_block(_join_chunks(bc_ref), _join_chunks(pooled_ref), sgc_ref[0], sgp_ref[0], ps_ref[...],
                      _weight(wco_ref, wco_c), _weight(wpg_ref, wpg_c))
    h2_ref[...] += _dot(m, _weight(wo_ref, wo_c))


def _outproj(bc, pooled, sgc, sgp, h, wco, wpg, pscale, wo, *, tn, name):
    bc_chunks, rows, bc_width = bc.shape
    d = bc_chunks * bc_width
    n_groups, pool_in, group_out = wpg.shape
    n_n = d // tn
    per_group = group_out // tn
    pool_chunks = pooled.shape[0] // n_groups
    assert d == n_n * tn and group_out == per_group * tn and d == n_groups * group_out
    assert sgc.shape == sgp.shape == (n_n, rows, tn) and pool_chunks * pooled.shape[2] == pool_in
    chunk_spec = lambda r: pl.BlockSpec((1, r, tn), lambda i, n: (n // per_group, 0, n % per_group))
    w_specs = [pl.BlockSpec((d, tn), lambda i, n: (0, n)), chunk_spec(pool_in),
               pl.BlockSpec((tn, d), lambda i, n: (n, 0))]
    w_shapes = [jax.ShapeDtypeStruct((n_groups, d, group_out), BF16),
                jax.ShapeDtypeStruct(wpg.shape, BF16), jax.ShapeDtypeStruct(wo.shape, BF16)]
    w_out_specs = [chunk_spec(d), w_specs[1], w_specs[2]]
    in_specs = [
        _resident((bc_chunks, rows, bc_width), lambda i, n: (0, 0, 0)),
        pl.BlockSpec((pool_chunks, rows, pooled.shape[2]), lambda i, n: (n // per_group, 0, 0)),
        pl.BlockSpec((1, rows, tn), lambda i, n: (n, 0, 0)),
        pl.BlockSpec((1, rows, tn), lambda i, n: (n, 0, 0)),
        _resident((rows, d), lambda i, n: (0, 0)),
        w_specs[0], w_specs[1],
        pl.BlockSpec((1, tn), lambda i, n: (0, n)),
        w_specs[2],
    ]
    return pl.pallas_call(
        _outproj_body,
        grid=(1, n_n),
        in_specs=in_specs,
        out_specs=[_resident((rows, d), lambda i, n: (0, 0))] + w_out_specs,
        out_shape=[jax.ShapeDtypeStruct((rows, d), F32)] + w_shapes,
        compiler_params=_params(("arbitrary", "arbitrary")),
        name=name,
    )(bc, pooled, sgc, sgp, h, wco, wpg, pscale, wo)


def _outproj_resident_body(bc_ref, pooled_ref, sgc_ref, sgp_ref, h_ref, wco_ref, wpg_ref, ps_ref, wo_ref,
                           h2_ref, *, groups):
    n = pl.program_id(1)
    tn = sgc_ref.shape[2]

    @pl.when(n == 0)
    def _():
        h2_ref[...] = h_ref[...]

    bc = _join_chunks(bc_ref)
    total = None
    for q in range(groups):
        k = n * groups + q
        m = _merged_block(bc, pooled_ref[q], sgc_ref[q], sgp_ref[q], ps_ref[:, q * tn:(q + 1) * tn],
                          wco_ref[k], wpg_ref[k])
        part = _dot(m, wo_ref[k])
        total = part if total is None else total + part
    h2_ref[...] += total


def _outproj_resident(bc, pooled, sgc, sgp, h, wco, wpg, pscale, wo, *, tm, groups, name):
    n_groups, rows, tn = sgc.shape
    d = n_groups * tn
    n_i, n_n = rows // tm, n_groups // groups
    assert rows == n_i * tm and n_groups == n_n * groups
    assert bc.shape == sgp.shape == (n_groups, rows, tn) and pooled.shape[:2] == (n_groups, rows)
    assert wco.shape == (n_groups, d, tn) and wo.shape == (n_groups, tn, d) and wpg.shape[0] == n_groups
    whole = lambda a: _resident(a.shape, lambda i, n: (0, 0, 0))
    per_step = lambda w: pl.BlockSpec((groups, tm, w), lambda i, n: (n, i, 0))
    in_specs = [
        pl.BlockSpec((n_groups, tm, tn), lambda i, n: (0, i, 0)),
        per_step(pooled.shape[2]), per_step(tn), per_step(tn),
        pl.BlockSpec((tm, d), lambda i, n: (i, 0)),
        whole(wco), whole(wpg),
        pl.BlockSpec((1, groups * tn), lambda i, n: (0, n)),
        whole(wo),
    ]
    return pl.pallas_call(
        functools.partial(_outproj_resident_body, groups=groups),
        grid=(n_i, n_n),
        in_specs=in_specs,
        out_specs=pl.BlockSpec((tm, d), lambda i, n: (i, 0)),
        out_shape=jax.ShapeDtypeStruct((rows, d), F32),
        compiler_params=_params(("arbitrary", "arbitrary")),
        name=name,
    )(bc, pooled, sgc, sgp, h, wco, wpg, pscale, wo)


def kernel(x_prompt, x_sample, state_conv, state_pool, meta_tokens, norm_ffn1, w_ffn1_gate,
           w_ffn1_up, w_ffn1_down, norm_mix, w_in, conv_w, w_conv_out, w_pool_group, pool_scale,
           w_o, norm_ffn2, w_ffn2_gate, w_ffn2_up, w_ffn2_down, norm_final):
    batch, seq, d = x_prompt.shape
    dec_batch, dec_seq, _ = x_sample.shape
    depth = w_in.shape[0]
    d_pool = state_pool.shape[-1]
    assert depth == 1 and dec_seq == F32_SUBLANES and meta_tokens.shape[0] == N_META == MAX_WIN
    assert state_conv.shape[2] == CONV_W - 1 and state_pool.shape[2] == MAX_WIN - 1
    n_s = dec_batch * dec_seq

    row = lambda v: v.reshape(1, d)
    g1, gm, g2, gf = row(norm_ffn1[0]), row(norm_mix[0]), row(norm_ffn2[0]), row(norm_final)
    pscale = row(pool_scale[0])
    cw = conv_w[0]

    h_s, u_s, wg1, wu1, wd1 = _ffn_cast((x_sample, meta_tokens.astype(F32)), g1,
                                        w_ffn1_gate[0], w_ffn1_up[0], w_ffn1_down[0], gm,
                                        tf=256, w_chunk=512, emit_h=True, name="ffn1_sample")
    zst = jnp.pad(state_pool[0], ((0, 0), (1, 0), (0, 0)))
    bc_s, pooled_s, sgc_s, sgp_s, cv_all, z_all, *w_groups = _mix_sample(u_s, w_in[0], cw, state_conv[0], zst, n_j=8)
    h2_s, wco, wpg, wo = _outproj(bc_s, pooled_s, sgc_s, sgp_s, h_s, w_conv_out[0], w_pool_group[0],
                                  pscale, w_o[0], tn=256, name="outproj_sample")
    y_s, wg2, wu2, wd2 = _ffn_cast((h2_s,), g2, w_ffn2_gate[0], w_ffn2_up[0], w_ffn2_down[0], gf,
                                   tf=256, w_chunk=512, emit_h=False, name="ffn2_sample")

    cv_s = cv_all[:n_s].reshape(dec_batch, dec_seq, d)
    new_conv_sample = cv_s[:, dec_seq - (CONV_W - 1):]
    new_pool_sample = jnp.concatenate(
        [state_pool[0][:, dec_seq:], z_all[:n_s].reshape(dec_batch, dec_seq, d_pool)], axis=1)

    cinit = jnp.pad(cv_all[n_s + N_META - (CONV_W - 1):], ((F32_SUBLANES - (CONV_W - 1), 0), (0, 0)))
    zinit = z_all[n_s:]
    xp = x_prompt.reshape(batch * seq, d)
    down_chunks = lambda wd, wg: wd.reshape(wg.shape[0], wg.shape[2], d)
    h_p, u_p = _ffn_stream(xp, g1, wg1, wu1, down_chunks(wd1, wg1), gm, tm=512, unit=2, emit_h=True,
                           name="ffn1_prompt")
    bc_p, pooled_p, sgc_p, sgp_p, cvlast, zlast = _mix_prompt(u_p, w_groups, cw, cinit, zinit,
                                                              n_seq=batch, tm=1024, n_j=4)
    tn = wpg.shape[2]
    h2_p = _outproj_resident(bc_p, pooled_p, sgc_p, sgp_p, h_p, wco, wpg, pscale, wo.reshape(d // tn, tn, d),
                             tm=512, groups=2, name="outproj_prompt")
    (y_p,) = _ffn_stream(h2_p, g2, wg2, wu2, down_chunks(wd2, wg2), gf, tm=512, unit=3, emit_h=False,
                         name="ffn2_prompt")

    return (y_p.reshape(batch, seq, d),
            y_s.reshape(dec_batch, dec_seq, d),
            cvlast[None],
            zlast[:, 1:][None],
            new_conv_sample[None],
            new_pool_sample[None])
```
